```python
import jax, jax.numpy as jnp
from jax import lax
import numpy as np

D_MODEL = 1024
BATCH = 8
SEQ = 2048
DEPTH = 1

CHUNK = 128
SGU_WIDTH = D_MODEL
SGU_GROUPS = 8
SGU_GROUP_DIM = SGU_WIDTH // SGU_GROUPS
CONV_WIDTH = D_MODEL
CONV_KERNEL = 31
N_BRANCH = 2
D_FF = 4 * D_MODEL
D_IN = 2 * SGU_WIDTH + 2 * CONV_WIDTH + N_BRANCH * D_MODEL
EPS = 1e-6

kernel_name = "hybrid_gmlp_conformer_gated_block"


def rmsnorm(x, g):
    xf = x.astype(jnp.float32)
    y = xf * lax.rsqrt(jnp.mean(xf * xf, axis=-1, keepdims=True) + EPS)
    return (y * g.astype(jnp.float32)).astype(x.dtype)


def layernorm(x, g, b):
    xf = x.astype(jnp.float32)
    mu = jnp.mean(xf, axis=-1, keepdims=True)
    var = jnp.mean(jnp.square(xf - mu), axis=-1, keepdims=True)
    y = (xf - mu) * lax.rsqrt(var + EPS)
    return (y * g.astype(jnp.float32) + b.astype(jnp.float32)).astype(x.dtype)


def chunked_causal_sgu(u, v, w_s, b_s):
    B, S, _ = v.shape
    n_chunks = S // CHUNK
    vc = v.reshape(B, n_chunks, CHUNK, SGU_GROUPS, SGU_GROUP_DIM)
    causal = jnp.tril(jnp.ones((CHUNK, CHUNK), dtype=bool))
    w = jnp.where(causal[None], w_s, jnp.zeros((), w_s.dtype))
    mixed = jnp.einsum('gts,bnsgc->bntgc', w, vc) + b_s.T[None, None, :, :, None]
    return u * mixed.reshape(B, S, SGU_WIDTH)


def causal_depthwise_conv(a, w, b):
    y = lax.conv_general_dilated(
        a, w[:, None, :], window_strides=(1,), padding=[(CONV_KERNEL - 1, 0)],
        dimension_numbers=('NWC', 'WIO', 'NWC'), feature_group_count=CONV_WIDTH)
    return y + b


def setup_inputs(seed: int = 0) -> dict:
    key = jax.random.key(seed)
    ks = jax.random.split(key, 20)
    f32 = jnp.float32
    L = DEPTH

    def nrm(k, shape, scale):
        return jax.random.normal(k, shape, f32) * scale

    return {
        "x": jax.random.normal(ks[0], (BATCH, SEQ, D_MODEL), f32),
        "norm1_g": 1.0 + nrm(ks[1], (L, D_MODEL), 0.02),
        "w_in": nrm(ks[2], (L, D_MODEL, D_IN), D_MODEL ** -0.5),
        "b_in": nrm(ks[3], (L, D_IN), 0.02),
        "sgu_ln_g": 1.0 + nrm(ks[4], (L, SGU_WIDTH), 0.02),
        "sgu_ln_b": nrm(ks[5], (L, SGU_WIDTH), 0.02),
        "sgu_w": nrm(ks[6], (L, SGU_GROUPS, CHUNK, CHUNK), CHUNK ** -0.5),
        "sgu_b": 1.0 + nrm(ks[7], (L, SGU_GROUPS, CHUNK), 0.02),
        "w_proj_a": nrm(ks[8], (L, SGU_WIDTH, D_MODEL), SGU_WIDTH ** -0.5),
        "conv_w": nrm(ks[9], (L, CONV_KERNEL, CONV_WIDTH), CONV_KERNEL ** -0.5),
        "conv_b": nrm(ks[10], (L, CONV_WIDTH), 0.02),
        "conv_ln_g": 1.0 + nrm(ks[11], (L, CONV_WIDTH), 0.02),
        "conv_ln_b": nrm(ks[12], (L, CONV_WIDTH), 0.02),
        "w_proj_b": nrm(ks[13], (L, CONV_WIDTH, D_MODEL), CONV_WIDTH ** -0.5),
        "b_proj_b": nrm(ks[14], (L, D_MODEL), 0.02),
        "w_out": nrm(ks[15], (L, D_MODEL, D_MODEL), D_MODEL ** -0.5),
        "norm2_g": 1.0 + nrm(ks[16], (L, D_MODEL), 0.02),
        "w_ff1": nrm(ks[17], (L, D_MODEL, D_FF), D_MODEL ** -0.5),
        "w_ff2": nrm(ks[18], (L, D_FF, D_MODEL), D_FF ** -0.5),
        "norm_f_g": 1.0 + nrm(ks[19], (D_MODEL,), 0.02),
    }


def reference(x, norm1_g, w_in, b_in, sgu_ln_g, sgu_ln_b, sgu_w, sgu_b, w_proj_a,
              conv_w, conv_b, conv_ln_g, conv_ln_b, w_proj_b, b_proj_b, w_out,
              norm2_g, w_ff1, w_ff2, norm_f_g):
    B, S, D = x.shape
    split_a = 2 * SGU_WIDTH
    split_b = split_a + 2 * CONV_WIDTH
    for l in range(DEPTH):
        h = rmsnorm(x, norm1_g[l])
        z = jnp.einsum('bsd,de->bse', h, w_in[l]) + b_in[l]
        z_a, z_b, z_g = z[..., :split_a], z[..., split_a:split_b], z[..., split_b:]

        z_a = jax.nn.gelu(z_a)
        u, v = z_a[..., :SGU_WIDTH], z_a[..., SGU_WIDTH:]
        v = layernorm(v, sgu_ln_g[l], sgu_ln_b[l])
        y_a = jnp.einsum('bsc,cd->bsd', chunked_causal_sgu(u, v, sgu_w[l], sgu_b[l]), w_proj_a[l])

        val, gate = z_b[..., :CONV_WIDTH], z_b[..., CONV_WIDTH:]
        a = val * jax.nn.sigmoid(gate)
        c = causal_depthwise_conv(a, conv_w[l], conv_b[l])
        c = jax.nn.silu(layernorm(c, conv_ln_g[l], conv_ln_b[l]))
        y_b = jnp.einsum('bsc,cd->bsd', c, w_proj_b[l]) + b_proj_b[l]

        g = jax.nn.sigmoid(z_g).reshape(B, S, N_BRANCH, D)
        merged = g[:, :, 0, :] * y_a + g[:, :, 1, :] * y_b
        x = x + jnp.einsum('bsd,de->bse', merged, w_out[l])

        h2 = rmsnorm(x, norm2_g[l])
        f = jnp.square(jax.nn.relu(jnp.einsum('bsd,df->bsf', h2, w_ff1[l])))
        x = x + jnp.einsum('bsf,fd->bsd', f, w_ff2[l])
    return rmsnorm(x, norm_f_g)
```

```python
import functools
import math

import jax
import jax.numpy as jnp
from jax import lax
from jax.experimental import pallas as pl
from jax.experimental.pallas import tpu as pltpu

CHUNK = 128
SGU_GROUPS = 8
CONV_KERNEL = 31
EPS = 1e-6

LANES = 128
HALO = 32
TOKENS = 256
COLS = 512
CONV_ROWS = 64
VMEM_CAPACITY = 64 * 1024 * 1024

_GELU_C = math.sqrt(2.0 / math.pi)


def _dot(a, b):
    return jnp.dot(a, b, preferred_element_type=jnp.float32)


def _sigmoid(x):
    return 0.5 * jnp.tanh(0.5 * x) + 0.5


def _gelu_tanh(x):
    return 0.5 * x * (1.0 + jnp.tanh(_GELU_C * (x + 0.044715 * (x * x * x))))


def _rmsnorm(x, g):
    return x * lax.rsqrt(jnp.mean(x * x, axis=-1, keepdims=True) + EPS) * g


def _layernorm(x, g, b):
    mu = jnp.mean(x, axis=-1, keepdims=True)
    xc = x - mu
    var = jnp.mean(xc * xc, axis=-1, keepdims=True)
    return xc * lax.rsqrt(var + EPS) * g + b


def _mixer_kernel(x_ref, n1g_ref, win_ref, bin_ref, lnag_ref, lnab_ref, sguw_ref, sgub_ref,
                  wpa_ref, convw_ref, convb_ref, lncg_ref, lncb_ref, wpb_ref, bpb_ref, wout_ref,
                  o_ref,
                  h_scr, u_scr, v_scr, vn_scr, sg_scr, ya_scr, aext_scr, c_scr, yb_scr, m_scr):
    T, D = h_scr.shape
    s = pl.program_id(1)

    h_scr[...] = _rmsnorm(x_ref[0], n1g_ref[...]).astype(jnp.bfloat16)

    def in_proj(col0):
        return _dot(h_scr[...], win_ref[:, col0:col0 + COLS]) + bin_ref[:, col0:col0 + COLS]

    for cb in range(D // COLS):
        c0 = cb * COLS
        u_scr[:, c0:c0 + COLS] = _gelu_tanh(in_proj(c0))
        v_scr[:, c0:c0 + COLS] = _gelu_tanh(in_proj(D + c0))

    vn_scr[...] = _layernorm(v_scr[...], lnag_ref[...], lnab_ref[...]).astype(jnp.bfloat16)

    row = lax.broadcasted_iota(jnp.int32, (CHUNK, CHUNK), 0)
    col = lax.broadcasted_iota(jnp.int32, (CHUNK, CHUNK), 1)
    causal = row >= col
    for g in range(SGU_GROUPS):
        wg = jnp.where(causal, sguw_ref[g], 0.0).astype(jnp.bfloat16)
        c0 = g * LANES
        for n in range(T // CHUNK):
            r0 = n * CHUNK
            mixed = _dot(wg, vn_scr[r0:r0 + CHUNK, c0:c0 + LANES]) + sgub_ref[g]
            sg_scr[r0:r0 + CHUNK, c0:c0 + LANES] = (
                u_scr[r0:r0 + CHUNK, c0:c0 + LANES] * mixed).astype(jnp.bfloat16)

    ya_scr[...] = _dot(sg_scr[...], wpa_ref[...])

    @pl.when(s == 0)
    def _():
        aext_scr[0:HALO, :] = jnp.zeros((HALO, D), jnp.float32)

    for cb in range(D // COLS):
        c0 = cb * COLS
        val = in_proj(2 * D + c0)
        gate = in_proj(3 * D + c0)
        aext_scr[HALO:HALO + T, c0:c0 + COLS] = val * _sigmoid(gate)

    tap0 = HALO - (CONV_KERNEL - 1)
    for lb in range(D // LANES):
        l0 = lb * LANES
        for rb in range(T // CONV_ROWS):
            r0 = rb * CONV_ROWS
            acc = jnp.zeros((CONV_ROWS, LANES), jnp.float32) + convb_ref[:, l0:l0 + LANES]
            for k in range(CONV_KERNEL):
                a_k = aext_scr[r0 + tap0 + k:r0 + tap0 + k + CONV_ROWS, l0:l0 + LANES]
                acc = acc + convw_ref[k:k + 1, l0:l0 + LANES] * a_k
            c_scr[r0:r0 + CONV_ROWS, l0:l0 + LANES] = acc

    aext_scr[0:HALO, :] = aext_scr[T:T + HALO, :]

    cn = _layernorm(c_scr[...], lncg_ref[...], lncb_ref[...])
    cn = cn * _sigmoid(cn)
    yb_scr[...] = _dot(cn.astype(jnp.bfloat16), wpb_ref[...]) + bpb_ref[...]

    for cb in range(D // COLS):
        c0 = cb * COLS
        g0 = _sigmoid(in_proj(4 * D + c0))
        g1 = _sigmoid(in_proj(5 * D + c0))
        m_scr[:, c0:c0 + COLS] = (g0 * ya_scr[:, c0:c0 + COLS]
                                  + g1 * yb_scr[:, c0:c0 + COLS]).astype(jnp.bfloat16)

    o_ref[0] = x_ref[0] + _dot(m_scr[...], wout_ref[...])


def _ffn_kernel(x_ref, n2g_ref, w1_ref, w2_ref, nfg_ref, o_ref, h_scr, f_scr, *, final_norm):
    D = h_scr.shape[1]
    F = f_scr.shape[1]
    x = x_ref[...]
    h_scr[...] = _rmsnorm(x, n2g_ref[...]).astype(jnp.bfloat16)
    for cb in range(F // COLS):
        c0 = cb * COLS
        f = jnp.maximum(_dot(h_scr[...], w1_ref[:, c0:c0 + COLS]), 0.0)
        f_scr[:, c0:c0 + COLS] = (f * f).astype(jnp.bfloat16)
    y = x + _dot(f_scr[...], w2_ref[...])
    if final_norm:
        y = _rmsnorm(y, nfg_ref[...])
    o_ref[...] = y


def _resident(shape):
    zeros = (0,) * len(shape)
    return pl.BlockSpec(shape, lambda *_: zeros, pipeline_mode=pl.Buffered(1))


def _nbytes(shape, dtype):
    return math.prod(shape) * jnp.dtype(dtype).itemsize


def _vmem_limit(resident, streamed, scratch, temporaries):
    total = sum(_nbytes(*b) for b in resident) + 2 * sum(_nbytes(*b) for b in streamed)
    total += sum(_nbytes(*b) for b in scratch) + sum(_nbytes(*b) for b in temporaries)
    assert total <= VMEM_CAPACITY, total
    return total


def _mixer(x, n1g, win, b_in, lnag, lnab, sguw, sgub, wpa, convw, convb, lncg, lncb, wpb, bpb, wout):
    B, S, D = x.shape
    T = TOKENS
    assert S % T == 0 and T % CHUNK == 0 and D % COLS == 0 and T % CONV_ROWS == 0
    f32, bf16 = jnp.float32, jnp.bfloat16
    params = (n1g, win, b_in, lnag, lnab, sguw, sgub, wpa, convw, convb, lncg, lncb, wpb, bpb, wout)
    scratch = [
        ((T, D), bf16),
        ((T, D), f32),
        ((T, D), f32),
        ((T, D), bf16),
        ((T, D), bf16),
        ((T, D), f32),
        ((HALO + T, D), f32),
        ((T, D), f32),
        ((T, D), f32),
        ((T, D), bf16),
    ]
    limit = _vmem_limit(
        resident=[(p.shape, p.dtype) for p in params],
        streamed=[((1, T, D), f32), ((1, T, D), f32)],
        scratch=scratch,
        temporaries=[((T, D), f32)] * 4)
    return pl.pallas_call(
        _mixer_kernel,
        grid=(B, S // T),
        in_specs=[pl.BlockSpec((1, T, D), lambda b, s: (b, s, 0))]
                 + [_resident(p.shape) for p in params],
        out_specs=pl.BlockSpec((1, T, D), lambda b, s: (b, s, 0)),
        out_shape=jax.ShapeDtypeStruct(x.shape, x.dtype),
        scratch_shapes=[pltpu.VMEM(shape, dtype) for shape, dtype in scratch],
        compiler_params=pltpu.CompilerParams(
            dimension_semantics=("arbitrary", "arbitrary"), vmem_limit_bytes=limit),
        name="mixer",
    )(x, *params)


def _ffn(x2d, n2g, w1, w2, nfg, final_norm):
    M, D = x2d.shape
    F = w1.shape[1]
    T = TOKENS
    assert M % T == 0 and F % COLS == 0
    f32, bf16 = jnp.float32, jnp.bfloat16
    params = (n2g, w1, w2, nfg)
    scratch = [((T, D), bf16), ((T, F), bf16)]
    limit = _vmem_limit(
        resident=[(p.shape, p.dtype) for p in params],
        streamed=[((T, D), f32), ((T, D), f32)],
        scratch=scratch,
        temporaries=[((T, D), f32)] * 4)
    return pl.pallas_call(
        functools.partial(_ffn_kernel, final_norm=final_norm),
        grid=(M // T,),
        in_specs=[pl.BlockSpec((T, D), lambda i: (i, 0))] + [_resident(p.shape) for p in params],
        out_specs=pl.BlockSpec((T, D), lambda i: (i, 0)),
        out_shape=jax.ShapeDtypeStruct(x2d.shape, x2d.dtype),
        scratch_shapes=[pltpu.VMEM(shape, dtype) for shape, dtype in scratch],
        compiler_params=pltpu.CompilerParams(
            dimension_semantics=("arbitrary",), vmem_limit_bytes=limit),
        name="ffn",
    )(x2d, *params)


def kernel(x, norm1_g, w_in, b_in, sgu_ln_g, sgu_ln_b, sgu_w, sgu_b, w_proj_a, conv_w, conv_b,
           conv_ln_g, conv_ln_b, w_proj_b, b_proj_b, w_out, norm2_g, w_ff1, w_ff2, norm_f_g):
    B, S, D = x.shape
    depth = w_in.shape[0]
    bf16 = jnp.bfloat16
    row = lambda p: p.reshape(1, -1)
    for l in range(depth):
        sgub = jnp.broadcast_to(sgu_b[l][:, :, None], (SGU_GROUPS, CHUNK, LANES))
        x = _mixer(x, row(norm1_g[l]), w_in[l].astype(bf16), row(b_in[l]),
                   row(sgu_ln_g[l]), row(sgu_ln_b[l]), sgu_w[l], sgub,
                   w_proj_a[l].astype(bf16), conv_w[l], row(conv_b[l]),
                   row(conv_ln_g[l]), row(conv_ln_b[l]), w_proj_b[l].astype(bf16),
                   row(b_proj_b[l]), w_out[l].astype(bf16))
        x = _ffn(x.reshape(B * S, D), row(norm2_g[l]), w_ff1[l].astype(bf16),
                 w_ff2[l].astype(bf16), row(norm_f_g), final_norm=(l == depth - 1)).reshape(B, S, D)
    return x
```

```python
import functools
import math

import jax
import jax.numpy as jnp
from jax import lax
from jax.experimental import pallas as pl
from jax.experimental.pallas import tpu as pltpu

CHUNK = 128
SGU_GROUPS = 8
CONV_KERNEL = 31
EPS = 1e-6

LANES = 128
HALO = 32
TOKENS = 256
COLS = 512
CONV_ROWS = 64
STAGE_ROWS, STAGE_COLS = 256, 1024
VMEM_CAPACITY = 64 * 1024 * 1024

_GELU_C = math.sqrt(2.0 / math.pi)


def _dot(a, b):
    return jnp.dot(a, b, preferred_element_type=jnp.float32)


def _sigmoid(x):
    return 0.5 * jnp.tanh(0.5 * x) + 0.5


def _gelu_tanh(x):
    return 0.5 * x * (1.0 + jnp.tanh(_GELU_C * (x + 0.044715 * (x * x * x))))


def _rmsnorm(x, g):
    return x * lax.rsqrt(jnp.mean(x * x, axis=-1, keepdims=True) + EPS) * g


def _layernorm(x, g, b):
    mu = jnp.mean(x, axis=-1, keepdims=True)
    xc = x - mu
    var = jnp.mean(xc * xc, axis=-1, keepdims=True)
    return xc * lax.rsqrt(var + EPS) * g + b


def _stage_weights(layer, pairs, stage_scr, sem):
    chunks = []
    for src, dst in pairs:
        rows, cols = dst.shape
        assert rows % STAGE_ROWS == 0 and cols % STAGE_COLS == 0
        for r0 in range(0, rows, STAGE_ROWS):
            for c0 in range(0, cols, STAGE_COLS):
                chunks.append((src, dst, r0, c0))

    def copy(j):
        src, _, r0, c0 = chunks[j]
        return pltpu.make_async_copy(
            src.at[layer, pl.ds(r0, STAGE_ROWS), pl.ds(c0, STAGE_COLS)],
            stage_scr.at[j % 2], sem.at[j % 2])

    copy(0).start()
    for j, (_, dst, r0, c0) in enumerate(chunks):
        if j + 1 < len(chunks):
            copy(j + 1).start()
        copy(j).wait()
        dst[r0:r0 + STAGE_ROWS, c0:c0 + STAGE_COLS] = stage_scr[j % 2].astype(jnp.bfloat16)


def _mixer_kernel(x_ref, n1g_ref, bin_ref, lnag_ref, lnab_ref, sguw_ref, sgub_ref,
                  convw_ref, convb_ref, lncg_ref, lncb_ref, bpb_ref,
                  win_hbm, wpa_hbm, wpb_hbm, wout_hbm,
                  o_ref,
                  win_scr, wpa_scr, wpb_scr, wout_scr, stage_scr, stage_sem,
                  h_scr, u_scr, v_scr, vn_scr, sg_scr, aext_scr, c_scr, cn_scr, g0_scr, g1_scr, m_scr,
                  *, layer):
    T, D = h_scr.shape
    bf16 = jnp.bfloat16

    @pl.when((pl.program_id(0) == 0) & (pl.program_id(1) == 0))
    def _():
        _stage_weights(layer, [(win_hbm, win_scr), (wpa_hbm, wpa_scr), (wpb_hbm, wpb_scr),
                               (wout_hbm, wout_scr)], stage_scr, stage_sem)

    @pl.when(pl.program_id(1) == 0)
    def _():
        aext_scr[:, 0:HALO, :] = jnp.zeros((D // LANES, HALO, LANES), jnp.float32)

    h_scr[...] = _rmsnorm(x_ref[0], n1g_ref[...]).astype(bf16)

    def in_proj(col0):
        return _dot(h_scr[...], win_scr[:, col0:col0 + COLS]) + bin_ref[:, col0:col0 + COLS]

    def glu_block(cb):
        c0 = cb * COLS
        a = in_proj(2 * D + c0) * _sigmoid(in_proj(3 * D + c0))
        for j in range(COLS // LANES):
            aext_scr[c0 // LANES + j, HALO:HALO + T, :] = a[:, j * LANES:(j + 1) * LANES]

    def conv_block(lb):
        tap0 = HALO - (CONV_KERNEL - 1)
        l0 = lb * LANES
        for rb in range(T // CONV_ROWS):
            r0 = rb * CONV_ROWS
            acc = jnp.zeros((CONV_ROWS, LANES), jnp.float32) + convb_ref[:, l0:l0 + LANES]
            for k in range(CONV_KERNEL):
                a_k = aext_scr[lb, pl.ds(r0 + tap0 + k, CONV_ROWS, stride=1), :]
                acc = acc + convw_ref[k:k + 1, l0:l0 + LANES] * a_k
            c_scr[r0:r0 + CONV_ROWS, l0:l0 + LANES] = acc

    def gelu_block(dst_scr, seg, cb):
        c0 = cb * COLS
        dst_scr[:, c0:c0 + COLS] = _gelu_tanh(in_proj(seg * D + c0))

    def gate_block(dst_scr, seg, cb):
        c0 = cb * COLS
        dst_scr[:, c0:c0 + COLS] = _sigmoid(in_proj(seg * D + c0))

    glu_block(0)
    glu_block(1)
    conv_block(0)
    conv_block(1)
    gelu_block(u_scr, 0, 0)
    conv_block(2)
    gelu_block(u_scr, 0, 1)
    conv_block(3)
    gelu_block(v_scr, 1, 0)
    conv_block(4)
    gelu_block(v_scr, 1, 1)
    conv_block(5)
    gate_block(g0_scr, 4, 0)
    conv_block(6)
    gate_block(g0_scr, 4, 1)
    conv_block(7)
    gate_block(g1_scr, 5, 0)
    gate_block(g1_scr, 5, 1)

    aext_scr[:, 0:HALO, :] = aext_scr[:, T:T + HALO, :]

    vn_scr[...] = _layernorm(v_scr[...], lnag_ref[...], lnab_ref[...]).astype(bf16)
    row = lax.broadcasted_iota(jnp.int32, (CHUNK, CHUNK), 0)
    col = lax.broadcasted_iota(jnp.int32, (CHUNK, CHUNK), 1)
    causal = row >= col
    n_chunks = T // CHUNK
    for g in range(SGU_GROUPS):
        wg = jnp.where(causal, sguw_ref[g], 0.0).astype(bf16)
        c0 = g * LANES
        vg = jnp.concatenate(
            [vn_scr[n * CHUNK:(n + 1) * CHUNK, c0:c0 + LANES] for n in range(n_chunks)], axis=1)
        mixed = _dot(wg, vg)
        for n in range(n_chunks):
            r0 = n * CHUNK
            sg_scr[r0:r0 + CHUNK, c0:c0 + LANES] = (
                u_scr[r0:r0 + CHUNK, c0:c0 + LANES]
                * (mixed[:, n * LANES:(n + 1) * LANES] + sgub_ref[g])).astype(bf16)

    cn = _layernorm(c_scr[...], lncg_ref[...], lncb_ref[...])
    cn_scr[...] = (cn * _sigmoid(cn)).astype(bf16)

    y_a = _dot(sg_scr[...], wpa_scr[...])
    y_b = _dot(cn_scr[...], wpb_scr[...]) + bpb_ref[...]
    m_scr[...] = (g0_scr[...] * y_a + g1_scr[...] * y_b).astype(bf16)
    o_ref[0] = x_ref[0] + _dot(m_scr[...], wout_scr[...])


def _ffn_kernel(x_ref, n2g_ref, nfg_ref, w1_hbm, w2_hbm, o_ref,
                w1_scr, w2_scr, stage_scr, stage_sem, h_scr, f_scr, *, layer, final_norm):
    F = f_scr.shape[1]

    @pl.when(pl.program_id(0) == 0)
    def _():
        _stage_weights(layer, [(w1_hbm, w1_scr), (w2_hbm, w2_scr)], stage_scr, stage_sem)

    x = x_ref[...]
    h_scr[...] = _rmsnorm(x, n2g_ref[...]).astype(jnp.bfloat16)
    for cb in range(F // COLS):
        c0 = cb * COLS
        f = jnp.maximum(_dot(h_scr[...], w1_scr[:, c0:c0 + COLS]), 0.0)
        f_scr[:, c0:c0 + COLS] = (f * f).astype(jnp.bfloat16)
    y = x + _dot(f_scr[...], w2_scr[...])
    if final_norm:
        y = _rmsnorm(y, nfg_ref[...])
    o_ref[...] = y


def _resident(shape):
    zeros = (0,) * len(shape)
    return pl.BlockSpec(shape, lambda *_: zeros, pipeline_mode=pl.Buffered(1))


def _nbytes(shape, dtype):
    return math.prod(shape) * jnp.dtype(dtype).itemsize


def _vmem_limit(resident, streamed, scratch, temporaries):
    total = sum(_nbytes(*b) for b in resident) + 2 * sum(_nbytes(*b) for b in streamed)
    total += sum(_nbytes(*b) for b in scratch) + sum(_nbytes(*b) for b in temporaries)
    assert total <= VMEM_CAPACITY, total
    return total


def _weight_scratch(weights):
    f32, bf16 = jnp.float32, jnp.bfloat16
    buffers = [(w.shape[1:], bf16) for w in weights] + [((2, STAGE_ROWS, STAGE_COLS), f32)]
    shapes = [pltpu.VMEM(shape, dtype) for shape, dtype in buffers] + [pltpu.SemaphoreType.DMA((2,))]
    return buffers, shapes


def _mixer(x, layer, n1g, b_in, lnag, lnab, sguw, sgub, convw, convb, lncg, lncb, bpb,
           w_in, w_proj_a, w_proj_b, w_out):
    B, S, D = x.shape
    T = TOKENS
    assert S % T == 0 and T % CHUNK == 0 and D % COLS == 0 and T % CONV_ROWS == 0
    f32, bf16 = jnp.float32, jnp.bfloat16
    params = (n1g, b_in, lnag, lnab, sguw, sgub, convw, convb, lncg, lncb, bpb)
    weights = (w_in, w_proj_a, w_proj_b, w_out)
    weight_buffers, weight_shapes = _weight_scratch(weights)
    scratch = [
        ((T, D), bf16),
        ((T, D), f32),
        ((T, D), f32),
        ((T, D), bf16),
        ((T, D), bf16),
        ((D // LANES, HALO + T, LANES), f32),
        ((T, D), f32),
        ((T, D), bf16),
        ((T, D), f32),
        ((T, D), f32),
        ((T, D), bf16),
    ]
    limit = _vmem_limit(
        resident=[(p.shape, p.dtype) for p in params],
        streamed=[((1, T, D), f32), ((1, T, D), f32)],
        scratch=weight_buffers + scratch,
        temporaries=[((T, D), f32)] * 4)
    return pl.pallas_call(
        functools.partial(_mixer_kernel, layer=layer),
        grid=(B, S // T),
        in_specs=[pl.BlockSpec((1, T, D), lambda b, s: (b, s, 0))]
                 + [_resident(p.shape) for p in params]
                 + [pl.BlockSpec(memory_space=pl.ANY)] * len(weights),
        out_specs=pl.BlockSpec((1, T, D), lambda b, s: (b, s, 0)),
        out_shape=jax.ShapeDtypeStruct(x.shape, x.dtype),
        scratch_shapes=weight_shapes + [pltpu.VMEM(shape, dtype) for shape, dtype in scratch],
        compiler_params=pltpu.CompilerParams(
            dimension_semantics=("arbitrary", "arbitrary"), vmem_limit_bytes=limit),
        name="mixer",
    )(x, *params, *weights)


def _ffn(x2d, layer, n2g, nfg, w_ff1, w_ff2, final_norm):
    M, D = x2d.shape
    F = w_ff1.shape[2]
    T = TOKENS
    assert M % T == 0 and F % COLS == 0
    f32, bf16 = jnp.float32, jnp.bfloat16
    params = (n2g, nfg)
    weights = (w_ff1, w_ff2)
    weight_buffers, weight_shapes = _weight_scratch(weights)
    scratch = [((T, D), bf16), ((T, F), bf16)]
    limit = _vmem_limit(
        resident=[(p.shape, p.dtype) for p in params],
        streamed=[((T, D), f32), ((T, D), f32)],
        scratch=weight_buffers + scratch,
        temporaries=[((T, D), f32)] * 4)
    return pl.pallas_call(
        functools.partial(_ffn_kernel, layer=layer, final_norm=final_norm),
        grid=(M // T,),
        in_specs=[pl.BlockSpec((T, D), lambda i: (i, 0))] + [_resident(p.shape) for p in params]
                 + [pl.BlockSpec(memory_space=pl.ANY)] * len(weights),
        out_specs=pl.BlockSpec((T, D), lambda i: (i, 0)),
        out_shape=jax.ShapeDtypeStruct(x2d.shape, x2d.dtype),
        scratch_shapes=weight_shapes + [pltpu.VMEM(shape, dtype) for shape, dtype in scratch],
        compiler_params=pltpu.CompilerParams(
            dimension_semantics=("arbitrary",), vmem_limit_bytes=limit),
        name="ffn",
    )(x2d, *params, *weights)


def kernel(x, norm1_g, w_in, b_in, sgu_ln_g, sgu_ln_b, sgu_w, sgu_b, w_proj_a, conv_w, conv_b,
           conv_ln_g, conv_ln_b, w_proj_b, b_proj_b, w_out, norm2_g, w_ff1, w_ff2, norm_f_g):
    B, S, D = x.shape
    depth = w_in.shape[0]
    row = lambda p: p.reshape(1, -1)
    for l in range(depth):
        sgub = jnp.broadcast_to(sgu_b[l][:, :, None], (SGU_GROUPS, CHUNK, LANES))
        x = _mixer(x, l, row(norm1_g[l]), row(b_in[l]), row(sgu_ln_g[l]), row(sgu_ln_b[l]),
                   sgu_w[l], sgub, conv_w[l], row(conv_b[l]), row(conv_ln_g[l]), row(conv_ln_b[l]),
                   row(b_proj_b[l]), w_in, w_proj_a, w_proj_b, w_out)
        x = _ffn(x.reshape(B * S, D), l, row(norm2_g[l]), row(norm_f_g), w_ff1, w_ff2,
                 final_norm=(l == depth - 1)).reshape(B, S, D)
    return x
```

```python
import functools
import math

import jax
import jax.numpy as jnp
from jax import lax
from jax.experimental import pallas as pl
from jax.experimental.pallas import tpu as pltpu

CHUNK = 128
SGU_GROUPS = 8
CONV_KERNEL = 31
EPS = 1e-6

LANES = 128
HALO = 32
TOKENS = 256
COLS = 512
CONV_ROWS = 64
STAGE_ROWS, STAGE_COLS = 256, 1024
VMEM_CAPACITY = 64 * 1024 * 1024

_GELU_C = math.sqrt(2.0 / math.pi)


def _dot(a, b):
    return jnp.dot(a, b, preferred_element_type=jnp.float32)


def _sigmoid(x):
    return 0.5 * jnp.tanh(0.5 * x) + 0.5


def _gelu_tanh(x):
    return 0.5 * x * (1.0 + jnp.tanh(_GELU_C * (x + 0.044715 * (x * x * x))))


def _rmsnorm(x, g):
    return x * lax.rsqrt(jnp.mean(x * x, axis=-1, keepdims=True) + EPS) * g


def _layernorm(x, g, b):
    mu = jnp.mean(x, axis=-1, keepdims=True)
    xc = x - mu
    var = jnp.mean(xc * xc, axis=-1, keepdims=True)
    return xc * lax.rsqrt(var + EPS) * g + b


def _stage_weights(layer, pairs, stage_scr, sem):
    chunks = []
    for src, dst in pairs:
        rows, cols = dst.shape
        assert rows % STAGE_ROWS == 0 and cols % STAGE_COLS == 0
        for r0 in range(0, rows, STAGE_ROWS):
            for c0 in range(0, cols, STAGE_COLS):
                chunks.append((src, dst, r0, c0))

    def copy(j):
        src, _, r0, c0 = chunks[j]
        return pltpu.make_async_copy(
            src.at[layer, pl.ds(r0, STAGE_ROWS), pl.ds(c0, STAGE_COLS)],
            stage_scr.at[j % 2], sem.at[j % 2])

    copy(0).start()
    for j, (_, dst, r0, c0) in enumerate(chunks):
        if j + 1 < len(chunks):
            copy(j + 1).start()
        copy(j).wait()
        dst[r0:r0 + STAGE_ROWS, c0:c0 + STAGE_COLS] = stage_scr[j % 2].astype(jnp.bfloat16)


def _block_kernel(x_ref, n1g_ref, bin_ref, lnag_ref, lnab_ref, sguw_ref, sgub_ref,
                  convw_ref, convb_ref, lncg_ref, lncb_ref, bpb_ref, n2g_ref, nfg_ref,
                  win_hbm, wpa_hbm, wpb_hbm, wout_hbm, w1_hbm, w2_hbm,
                  o_ref,
                  win_scr, wpa_scr, wpb_scr, wout_scr, w1_scr, w2_scr, stage_scr, stage_sem,
                  h_scr, u_scr, v_scr, vn_scr, sg_scr, aext_scr, c_scr, cn_scr, g0_scr, g1_scr, m_scr,
                  x1_scr, x1p_scr, h2_scr, f_scr,
                  *, layer, final_norm, n_tiles, tiles_per_seq):
    T, D = h_scr.shape
    F = f_scr.shape[1]
    bf16 = jnp.bfloat16
    step = pl.program_id(0)

    @pl.when(step == 0)
    def _():
        _stage_weights(layer, [(win_hbm, win_scr), (w1_hbm, w1_scr), (wpa_hbm, wpa_scr),
                               (wpb_hbm, wpb_scr), (wout_hbm, wout_scr), (w2_hbm, w2_scr)],
                       stage_scr, stage_sem)
        x1_scr[...] = jnp.zeros(x1_scr.shape, x1_scr.dtype)
        h2_scr[...] = jnp.zeros(h2_scr.shape, h2_scr.dtype)

    @pl.when(lax.rem(jnp.minimum(step, n_tiles - 1), tiles_per_seq) == 0)
    def _():
        aext_scr[:, 0:HALO, :] = jnp.zeros((D // LANES, HALO, LANES), jnp.float32)

    def ffn_up(cb):
        c0 = cb * COLS
        f = jnp.maximum(_dot(h2_scr[...], w1_scr[:, c0:c0 + COLS]), 0.0)
        f_scr[:, c0:c0 + COLS] = (f * f).astype(bf16)

    def ffn_down(cb):
        c0 = cb * COLS
        o_ref[:, c0:c0 + COLS] = x1p_scr[:, c0:c0 + COLS] + _dot(f_scr[...], w2_scr[:, c0:c0 + COLS])

    def ffn_finish():
        if final_norm:
            o_ref[...] = _rmsnorm(o_ref[...], nfg_ref[...])

    def mixer_start():
        h_scr[...] = _rmsnorm(x_ref[...], n1g_ref[...]).astype(bf16)

    def in_proj(col0):
        return _dot(h_scr[...], win_scr[:, col0:col0 + COLS]) + bin_ref[:, col0:col0 + COLS]

    def glu_block(cb):
        c0 = cb * COLS
        a = in_proj(2 * D + c0) * _sigmoid(in_proj(3 * D + c0))
        for j in range(COLS // LANES):
            aext_scr[c0 // LANES + j, HALO:HALO + T, :] = a[:, j * LANES:(j + 1) * LANES]

    def conv_block(lb):
        tap0 = HALO - (CONV_KERNEL - 1)
        l0 = lb * LANES
        for rb in range(T // CONV_ROWS):
            r0 = rb * CONV_ROWS
            acc = jnp.zeros((CONV_ROWS, LANES), jnp.float32) + convb_ref[:, l0:l0 + LANES]
            for k in range(CONV_KERNEL):
                a_k = aext_scr[lb, pl.ds(r0 + tap0 + k, CONV_ROWS, stride=1), :]
                acc = acc + convw_ref[k:k + 1, l0:l0 + LANES] * a_k
            c_scr[r0:r0 + CONV_ROWS, l0:l0 + LANES] = acc

    def gelu_block(dst_scr, seg, cb):
        c0 = cb * COLS
        dst_scr[:, c0:c0 + COLS] = _gelu_tanh(in_proj(seg * D + c0))

    def gate_block(dst_scr, seg, cb):
        c0 = cb * COLS
        dst_scr[:, c0:c0 + COLS] = _sigmoid(in_proj(seg * D + c0))

    def halo_keep():
        aext_scr[:, 0:HALO, :] = aext_scr[:, T:T + HALO, :]

    def spatial_gating():
        vn_scr[...] = _layernorm(v_scr[...], lnag_ref[...], lnab_ref[...]).astype(bf16)
        row = lax.broadcasted_iota(jnp.int32, (CHUNK, CHUNK), 0)
        col = lax.broadcasted_iota(jnp.int32, (CHUNK, CHUNK), 1)
        causal = row >= col
        n_chunks = T // CHUNK
        for g in range(SGU_GROUPS):
            wg = jnp.where(causal, sguw_ref[g], 0.0).astype(bf16)
            c0 = g * LANES
            vg = jnp.concatenate(
                [vn_scr[n * CHUNK:(n + 1) * CHUNK, c0:c0 + LANES] for n in range(n_chunks)], axis=1)
            mixed = _dot(wg, vg)
            for n in range(n_chunks):
                r0 = n * CHUNK
                sg_scr[r0:r0 + CHUNK, c0:c0 + LANES] = (
                    u_scr[r0:r0 + CHUNK, c0:c0 + LANES]
                    * (mixed[:, n * LANES:(n + 1) * LANES] + sgub_ref[g])).astype(bf16)

    def conv_tail():
        cn = _layernorm(c_scr[...], lncg_ref[...], lncb_ref[...])
        cn_scr[...] = (cn * _sigmoid(cn)).astype(bf16)

    def merge():
        y_a = _dot(sg_scr[...], wpa_scr[...])
        y_b = _dot(cn_scr[...], wpb_scr[...]) + bpb_ref[...]
        m_scr[...] = (g0_scr[...] * y_a + g1_scr[...] * y_b).astype(bf16)

    def mixer_finish():
        x1 = x_ref[...] + _dot(m_scr[...], wout_scr[...])
        x1_scr[...] = x1
        h2_scr[...] = _rmsnorm(x1, n2g_ref[...]).astype(bf16)

    x1p_scr[...] = x1_scr[...]
    ffn_up(0)
    mixer_start()
    ffn_up(1)
    glu_block(0)
    ffn_up(2)
    glu_block(1)
    conv_block(0)
    ffn_up(3)
    conv_block(1)
    gelu_block(u_scr, 0, 0)
    conv_block(2)
    ffn_up(4)
    conv_block(3)
    gelu_block(u_scr, 0, 1)
    conv_block(4)
    ffn_up(5)
    conv_block(5)
    gelu_block(v_scr, 1, 0)
    conv_block(6)
    ffn_up(6)
    conv_block(7)
    gelu_block(v_scr, 1, 1)
    ffn_up(7)
    halo_keep()
    gate_block(g0_scr, 4, 0)
    spatial_gating()
    gate_block(g0_scr, 4, 1)
    conv_tail()
    ffn_down(0)
    gate_block(g1_scr, 5, 0)
    gate_block(g1_scr, 5, 1)
    merge()
    ffn_down(1)
    mixer_finish()
    ffn_finish()


def _resident(shape):
    zeros = (0,) * len(shape)
    return pl.BlockSpec(shape, lambda *_: zeros, pipeline_mode=pl.Buffered(1))


def _nbytes(shape, dtype):
    return math.prod(shape) * jnp.dtype(dtype).itemsize


def _vmem_limit(resident, streamed, scratch, temporaries):
    total = sum(_nbytes(*b) for b in resident) + 2 * sum(_nbytes(*b) for b in streamed)
    total += sum(_nbytes(*b) for b in scratch) + sum(_nbytes(*b) for b in temporaries)
    assert total <= VMEM_CAPACITY, total
    return total


def _block(x2d, layer, tiles_per_seq, final_norm, params, weights):
    M, D = x2d.shape
    F = weights[4].shape[2]
    T = TOKENS
    assert M % T == 0 and T % CHUNK == 0 and D % COLS == 0 and F % COLS == 0 and T % CONV_ROWS == 0
    n_tiles = M // T
    f32, bf16 = jnp.float32, jnp.bfloat16
    weight_buffers = [(w.shape[1:], bf16) for w in weights] + [((2, STAGE_ROWS, STAGE_COLS), f32)]
    scratch = [
        ((T, D), bf16),
        ((T, D), f32),
        ((T, D), f32),
        ((T, D), bf16),
        ((T, D), bf16),
        ((D // LANES, HALO + T, LANES), f32),
        ((T, D), f32),
        ((T, D), bf16),
        ((T, D), f32),
        ((T, D), f32),
        ((T, D), bf16),
        ((T, D), f32),
        ((T, D), f32),
        ((T, D), bf16),
        ((T, F), bf16),
    ]
    limit = _vmem_limit(
        resident=[(p.shape, p.dtype) for p in params],
        streamed=[((T, D), f32), ((T, D), f32)],
        scratch=weight_buffers + scratch,
        temporaries=[((T, D), f32)] * 2)
    return pl.pallas_call(
        functools.partial(_block_kernel, layer=layer, final_norm=final_norm, n_tiles=n_tiles,
                          tiles_per_seq=tiles_per_seq),
        grid=(n_tiles + 1,),
        in_specs=[pl.BlockSpec((T, D), lambda i: (jnp.minimum(i, n_tiles - 1), 0))]
                 + [_resident(p.shape) for p in params]
                 + [pl.BlockSpec(memory_space=pl.ANY)] * len(weights),
        out_specs=pl.BlockSpec((T, D), lambda i: (jnp.maximum(i - 1, 0), 0)),
        out_shape=jax.ShapeDtypeStruct(x2d.shape, x2d.dtype),
        scratch_shapes=[pltpu.VMEM(shape, dtype) for shape, dtype in weight_buffers]
                       + [pltpu.SemaphoreType.DMA((2,))]
                       + [pltpu.VMEM(shape, dtype) for shape, dtype in scratch],
        compiler_params=pltpu.CompilerParams(
            dimension_semantics=("arbitrary",), vmem_limit_bytes=limit),
        name="block",
    )(x2d, *params, *weights)


def kernel(x, norm1_g, w_in, b_in, sgu_ln_g, sgu_ln_b, sgu_w, sgu_b, w_proj_a, conv_w, conv_b,
           conv_ln_g, conv_ln_b, w_proj_b, b_proj_b, w_out, norm2_g, w_ff1, w_ff2, norm_f_g):
    B, S, D = x.shape
    depth = w_in.shape[0]
    row = lambda p: p.reshape(1, -1)
    x2d = x.reshape(B * S, D)
    for l in range(depth):
        sgub = jnp.broadcast_to(sgu_b[l][:, :, None], (SGU_GROUPS, CHUNK, LANES))
        params = (row(norm1_g[l]), row(b_in[l]), row(sgu_ln_g[l]), row(sgu_ln_b[l]), sgu_w[l], sgub,
                  conv_w[l], row(conv_b[l]), row(conv_ln_g[l]), row(conv_ln_b[l]), row(b_proj_b[l]),
                  row(norm2_g[l]), row(norm_f_g))
        x2d = _block(x2d, l, S // TOKENS, l == depth - 1, params,
                     (w_in, w_proj_a, w_proj_b, w_out, w_ff1, w_ff2))
    return x2d.reshape(B, S, D)
```

```python
import functools
import math

import jax
import jax.numpy as jnp
from jax import lax
from jax.experimental import pallas as pl
from jax.experimental.pallas import tpu as pltpu

CHUNK = 128
SGU_GROUPS = 8
CONV_KERNEL = 31
EPS = 1e-6

LANES = 128
SUBLANES = 8
HALO = 32
TOKENS = 256
COLS = 512
CONV_ROWS = 64
STAGE_ROWS, STAGE_COLS = 256, 1024
VMEM_CAPACITY = 64 * 1024 * 1024

_GELU_C = math.sqrt(2.0 / math.pi)


def _dot(a, b):
    return jnp.dot(a, b, preferred_element_type=jnp.float32)


def _sigmoid(x):
    return 0.5 * jnp.tanh(0.5 * x) + 0.5


def _gelu_tanh(x):
    return 0.5 * x * (1.0 + jnp.tanh(_GELU_C * (x + 0.044715 * (x * x * x))))


def _rmsnorm(x, g):
    return x * lax.rsqrt(jnp.mean(x * x, axis=-1, keepdims=True) + EPS) * g


def _layernorm(x, g, b):
    mu = jnp.mean(x, axis=-1, keepdims=True)
    xc = x - mu
    var = jnp.mean(xc * xc, axis=-1, keepdims=True)
    return xc * lax.rsqrt(var + EPS) * g + b


def _stage_weights(layer, pairs, stage_scr, sem):
    chunks = []
    for src, dst in pairs:
        rows, cols = dst.shape
        assert rows % STAGE_ROWS == 0 and cols % STAGE_COLS == 0
        for r0 in range(0, rows, STAGE_ROWS):
            for c0 in range(0, cols, STAGE_COLS):
                chunks.append((src, dst, r0, c0))

    def copy(j):
        src, _, r0, c0 = chunks[j]
        return pltpu.make_async_copy(
            src.at[layer, pl.ds(r0, STAGE_ROWS), pl.ds(c0, STAGE_COLS)],
            stage_scr.at[j % 2], sem.at[j % 2])

    copy(0).start()
    for j, (_, dst, r0, c0) in enumerate(chunks):
        if j + 1 < len(chunks):
            copy(j + 1).start()
        copy(j).wait()
        dst[r0:r0 + STAGE_ROWS, c0:c0 + STAGE_COLS] = stage_scr[j % 2].astype(jnp.bfloat16)


def _block_kernel(x_ref, n1g_ref, bin_ref, lnag_ref, lnab_ref, sguw_ref, sgub_ref,
                  convw_ref, convb_ref, lncg_ref, lncb_ref, bpb_ref, n2g_ref, nfg_ref,
                  win_hbm, wpa_hbm, wpb_hbm, wout_hbm, w1_hbm, w2_hbm,
                  o_ref,
                  win_scr, wpa_scr, wpb_scr, wout_scr, w1_scr, w2_scr, stage_scr, stage_sem,
                  h_scr, u_scr, v_scr, vn_scr, sg_scr, aext_scr, c_scr, cn_scr, g0_scr, g1_scr, m_scr,
                  x1_scr, x1p_scr, h2_scr, f_scr, convw8_scr,
                  *, layer, final_norm, n_tiles, tiles_per_seq):
    T, D = h_scr.shape
    F = f_scr.shape[1]
    bf16 = jnp.bfloat16
    step = pl.program_id(0)

    @pl.when(step == 0)
    def _():
        _stage_weights(layer, [(win_hbm, win_scr), (w1_hbm, w1_scr), (wpa_hbm, wpa_scr),
                               (wpb_hbm, wpb_scr), (wout_hbm, wout_scr), (w2_hbm, w2_scr)],
                       stage_scr, stage_sem)
        x1_scr[...] = jnp.zeros(x1_scr.shape, x1_scr.dtype)
        h2_scr[...] = jnp.zeros(h2_scr.shape, h2_scr.dtype)
        for k in range(CONV_KERNEL):
            convw8_scr[k] = jnp.broadcast_to(convw_ref[k:k + 1, :], (SUBLANES, D))

    @pl.when(lax.rem(jnp.minimum(step, n_tiles - 1), tiles_per_seq) == 0)
    def _():
        aext_scr[:, 0:HALO, :] = jnp.zeros((D // LANES, HALO, LANES), jnp.float32)

    def ffn_up(cb):
        c0 = cb * COLS
        f = jnp.maximum(_dot(h2_scr[...], w1_scr[:, c0:c0 + COLS]), 0.0)
        f_scr[:, c0:c0 + COLS] = (f * f).astype(bf16)

    def ffn_down(cb):
        c0 = cb * COLS
        o_ref[:, c0:c0 + COLS] = x1p_scr[:, c0:c0 + COLS] + _dot(f_scr[...], w2_scr[:, c0:c0 + COLS])

    def ffn_finish():
        if final_norm:
            o_ref[...] = _rmsnorm(o_ref[...], nfg_ref[...])

    def mixer_start():
        h_scr[...] = _rmsnorm(x_ref[...], n1g_ref[...]).astype(bf16)

    def in_proj(col0):
        return _dot(h_scr[...], win_scr[:, col0:col0 + COLS]) + bin_ref[:, col0:col0 + COLS]

    def glu_block(cb):
        c0 = cb * COLS
        a = in_proj(2 * D + c0) * _sigmoid(in_proj(3 * D + c0))
        for j in range(COLS // LANES):
            aext_scr[c0 // LANES + j, HALO:HALO + T, :] = a[:, j * LANES:(j + 1) * LANES]

    def conv_block(lb):
        tap0 = HALO - (CONV_KERNEL - 1)
        l0 = lb * LANES
        for rb in range(T // CONV_ROWS):
            r0 = rb * CONV_ROWS
            acc = jnp.zeros((CONV_ROWS, LANES), jnp.float32) + convb_ref[:, l0:l0 + LANES]
            for k in range(CONV_KERNEL):
                a_k = aext_scr[lb, pl.ds(r0 + tap0 + k, CONV_ROWS, stride=1), :]
                w_k = jnp.concatenate([convw8_scr[k, :, l0:l0 + LANES]] * (CONV_ROWS // SUBLANES), axis=0)
                acc = acc + w_k * a_k
            c_scr[r0:r0 + CONV_ROWS, l0:l0 + LANES] = acc

    def gelu_block(dst_scr, seg, cb):
        c0 = cb * COLS
        dst_scr[:, c0:c0 + COLS] = _gelu_tanh(in_proj(seg * D + c0))

    def gate_block(dst_scr, seg, cb):
        c0 = cb * COLS
        dst_scr[:, c0:c0 + COLS] = _sigmoid(in_proj(seg * D + c0))

    def halo_keep():
        aext_scr[:, 0:HALO, :] = aext_scr[:, T:T + HALO, :]

    def spatial_gating():
        vn_scr[...] = _layernorm(v_scr[...], lnag_ref[...], lnab_ref[...]).astype(bf16)
        row = lax.broadcasted_iota(jnp.int32, (CHUNK, CHUNK), 0)
        col = lax.broadcasted_iota(jnp.int32, (CHUNK, CHUNK), 1)
        causal = row >= col
        n_chunks = T // CHUNK
        for g in range(SGU_GROUPS):
            wg = jnp.where(causal, sguw_ref[g], 0.0).astype(bf16)
            c0 = g * LANES
            vg = jnp.concatenate(
                [vn_scr[n * CHUNK:(n + 1) * CHUNK, c0:c0 + LANES] for n in range(n_chunks)], axis=1)
            mixed = _dot(wg, vg)
            for n in range(n_chunks):
                r0 = n * CHUNK
                sg_scr[r0:r0 + CHUNK, c0:c0 + LANES] = (
                    u_scr[r0:r0 + CHUNK, c0:c0 + LANES]
                    * (mixed[:, n * LANES:(n + 1) * LANES] + sgub_ref[g])).astype(bf16)

    def conv_tail():
        cn = _layernorm(c_scr[...], lncg_ref[...], lncb_ref[...])
        cn_scr[...] = (cn * _sigmoid(cn)).astype(bf16)

    def merge():
        y_a = _dot(sg_scr[...], wpa_scr[...])
        y_b = _dot(cn_scr[...], wpb_scr[...]) + bpb_ref[...]
        m_scr[...] = (g0_scr[...] * y_a + g1_scr[...] * y_b).astype(bf16)

    def mixer_finish():
        x1 = x_ref[...] + _dot(m_scr[...], wout_scr[...])
        x1_scr[...] = x1
        h2_scr[...] = _rmsnorm(x1, n2g_ref[...]).astype(bf16)

    x1p_scr[...] = x1_scr[...]
    ffn_up(0)
    mixer_start()
    ffn_up(1)
    glu_block(0)
    ffn_up(2)
    glu_block(1)
    conv_block(0)
    ffn_up(3)
    conv_block(1)
    gelu_block(u_scr, 0, 0)
    conv_block(2)
    ffn_up(4)
    conv_block(3)
    gelu_block(u_scr, 0, 1)
    conv_block(4)
    ffn_up(5)
    conv_block(5)
    gelu_block(v_scr, 1, 0)
    conv_block(6)
    ffn_up(6)
    conv_block(7)
    gelu_block(v_scr, 1, 1)
    ffn_up(7)
    halo_keep()
    gate_block(g0_scr, 4, 0)
    spatial_gating()
    gate_block(g0_scr, 4, 1)
    conv_tail()
    ffn_down(0)
    gate_block(g1_scr, 5, 0)
    gate_block(g1_scr, 5, 1)
    merge()
    ffn_down(1)
    mixer_finish()
    ffn_finish()


def _resident(shape):
    zeros = (0,) * len(shape)
    return pl.BlockSpec(shape, lambda *_: zeros, pipeline_mode=pl.Buffered(1))


def _nbytes(shape, dtype):
    return math.prod(shape) * jnp.dtype(dtype).itemsize


def _vmem_limit(resident, streamed, scratch, temporaries):
    total = sum(_nbytes(*b) for b in resident) + 2 * sum(_nbytes(*b) for b in streamed)
    total += sum(_nbytes(*b) for b in scratch) + sum(_nbytes(*b) for b in temporaries)
    assert total <= VMEM_CAPACITY, total
    return total


def _block(x2d, layer, tiles_per_seq, final_norm, params, weights):
    M, D = x2d.shape
    F = weights[4].shape[2]
    T = TOKENS
    assert M % T == 0 and T % CHUNK == 0 and D % COLS == 0 and F % COLS == 0 and T % CONV_ROWS == 0
    n_tiles = M // T
    f32, bf16 = jnp.float32, jnp.bfloat16
    weight_buffers = [(w.shape[1:], bf16) for w in weights] + [((2, STAGE_ROWS, STAGE_COLS), f32)]
    scratch = [
        ((T, D), bf16),
        ((T, D), f32),
        ((T, D), f32),
        ((T, D), bf16),
        ((T, D), bf16),
        ((D // LANES, HALO + T, LANES), f32),
        ((T, D), f32),
        ((T, D), bf16),
        ((T, D), f32),
        ((T, D), f32),
        ((T, D), bf16),
        ((T, D), f32),
        ((T, D), f32),
        ((T, D), bf16),
        ((T, F), bf16),
        ((CONV_KERNEL, SUBLANES, D), f32),
    ]
    limit = _vmem_limit(
        resident=[(p.shape, p.dtype) for p in params],
        streamed=[((T, D), f32), ((T, D), f32)],
        scratch=weight_buffers + scratch,
        temporaries=[((T, D), f32)] * 2)
    return pl.pallas_call(
        functools.partial(_block_kernel, layer=layer, final_norm=final_norm, n_tiles=n_tiles,
                          tiles_per_seq=tiles_per_seq),
        grid=(n_tiles + 1,),
        in_specs=[pl.BlockSpec((T, D), lambda i: (jnp.minimum(i, n_tiles - 1), 0))]
                 + [_resident(p.shape) for p in params]
                 + [pl.BlockSpec(memory_space=pl.ANY)] * len(weights),
        out_specs=pl.BlockSpec((T, D), lambda i: (jnp.maximum(i - 1, 0), 0)),
        out_shape=jax.ShapeDtypeStruct(x2d.shape, x2d.dtype),
        scratch_shapes=[pltpu.VMEM(shape, dtype) for shape, dtype in weight_buffers]
                       + [pltpu.SemaphoreType.DMA((2,))]
                       + [pltpu.VMEM(shape, dtype) for shape, dtype in scratch],
        compiler_params=pltpu.CompilerParams(
            dimension_semantics=("arbitrary",), vmem_limit_bytes=limit),
        name="block",
    )(x2d, *params, *weights)


def kernel(x, norm1_g, w_in, b_in, sgu_ln_g, sgu_ln_b, sgu_w, sgu_b, w_proj_a, conv_w, conv_b,
           conv_ln_g, conv_ln_b, w_proj_b, b_proj_b, w_out, norm2_g, w_ff1, w_ff2, norm_f_g):
    B, S, D = x.shape
    depth = w_in.shape[0]
    row = lambda p: p.reshape(1, -1)
    x2d = x.reshape(B * S, D)
    for l in range(depth):
        sgub = jnp.broadcast_to(sgu_b[l][:, :, None], (SGU_GROUPS, CHUNK, LANES))
        params = (row(norm1_g[l]), row(b_in[l]), row(sgu_ln_g[l]), row(sgu_ln_b[l]), sgu_w[l], sgub,
                  conv_w[l], row(conv_b[l]), row(conv_ln_g[l]), row(conv_ln_b[l]), row(b_proj_b[l]),
                  row(norm2_g[l]), row(norm_f_g))
        x2d = _block(x2d, l, S // TOKENS, l == depth - 1, params,
                     (w_in, w_proj_a, w_proj_b, w_out, w_ff1, w_ff2))
    return x2d.reshape(B, S, D)
```

```python
import functools
import math

import jax
import jax.numpy as jnp
from jax import lax
from jax.experimental import pallas as pl
from jax.experimental.pallas import tpu as pltpu

CHUNK = 128
SGU_GROUPS = 8
CONV_KERNEL = 31
EPS = 1e-6

LANES = 128
HALO = 32
TOKENS = 256
COLS = 512
CONV_ROWS = 64
STAGE_ROWS, STAGE_COLS = 128, 1024
STAGE_SLOTS = 4
VMEM_CAPACITY = 64 * 1024 * 1024

_GELU_C = math.sqrt(2.0 / math.pi)


def _dot(a, b):
    return jnp.dot(a, b, preferred_element_type=jnp.float32)


def _sigmoid(x):
    return 0.5 * jnp.tanh(0.5 * x) + 0.5


def _gelu_tanh(x):
    return 0.5 * x * (1.0 + jnp.tanh(_GELU_C * (x + 0.044715 * (x * x * x))))


def _rmsnorm(x, g):
    return x * lax.rsqrt(jnp.mean(x * x, axis=-1, keepdims=True) + EPS) * g


def _layernorm(x, g, b):
    mu = jnp.mean(x, axis=-1, keepdims=True)
    xc = x - mu
    var = jnp.mean(xc * xc, axis=-1, keepdims=True)
    return xc * lax.rsqrt(var + EPS) * g + b


def _stage_weights(layer, pairs, stage_scr, sem):
    chunks = []
    for src, dst in pairs:
        rows, cols = dst.shape
        assert rows % STAGE_ROWS == 0 and cols % STAGE_COLS == 0
        for r0 in range(0, rows, STAGE_ROWS):
            for c0 in range(0, cols, STAGE_COLS):
                chunks.append((src, dst, r0, c0))

    def copy(j):
        src, _, r0, c0 = chunks[j]
        return pltpu.make_async_copy(
            src.at[layer, pl.ds(r0, STAGE_ROWS), pl.ds(c0, STAGE_COLS)],
            stage_scr.at[j % STAGE_SLOTS], sem.at[j % STAGE_SLOTS])

    lookahead = STAGE_SLOTS - 1
    for j in range(min(lookahead, len(chunks))):
        copy(j).start()
    for j, (_, dst, r0, c0) in enumerate(chunks):
        if j + lookahead < len(chunks):
            copy(j + lookahead).start()
        copy(j).wait()
        dst[r0:r0 + STAGE_ROWS, c0:c0 + STAGE_COLS] = stage_scr[j % STAGE_SLOTS].astype(jnp.bfloat16)


def _block_kernel(x_ref, n1g_ref, bin_ref, lnag_ref, lnab_ref, sguw_ref, sgub_ref,
                  convw_ref, convb_ref, lncg_ref, lncb_ref, bpb_ref, n2g_ref, nfg_ref,
                  win_hbm, wpa_hbm, wpb_hbm, wout_hbm, w1_hbm, w2_hbm,
                  o_ref,
                  win_scr, wpa_scr, wpb_scr, wout_scr, w1_scr, w2_scr, stage_scr, stage_sem,
                  h_scr, u_scr, v_scr, vn_scr, sg_scr, aext_scr, c_scr, cn_scr, g0_scr, g1_scr, m_scr,
                  x1_scr, x1p_scr, h2_scr, f_scr,
                  *, layer, final_norm, n_tiles, tiles_per_seq):
    T, D = h_scr.shape
    bf16 = jnp.bfloat16
    step = pl.program_id(0)
    assert D // COLS == 2 and D // LANES == 8 and f_scr.shape[1] // COLS == 8

    @pl.when(step == 0)
    def _():
        _stage_weights(layer, [(win_hbm, win_scr), (w1_hbm, w1_scr), (wpa_hbm, wpa_scr),
                               (wpb_hbm, wpb_scr), (wout_hbm, wout_scr), (w2_hbm, w2_scr)],
                       stage_scr, stage_sem)
        x1_scr[...] = jnp.zeros(x1_scr.shape, x1_scr.dtype)
        h2_scr[...] = jnp.zeros(h2_scr.shape, h2_scr.dtype)

    @pl.when(lax.rem(jnp.minimum(step, n_tiles - 1), tiles_per_seq) == 0)
    def _():
        aext_scr[:, 0:HALO, :] = jnp.zeros((D // LANES, HALO, LANES), jnp.float32)

    def ffn_up(cb):
        c0 = cb * COLS
        f = jnp.maximum(_dot(h2_scr[...], w1_scr[:, c0:c0 + COLS]), 0.0)
        f_scr[:, c0:c0 + COLS] = (f * f).astype(bf16)

    def ffn_down(cb):
        c0 = cb * COLS
        o_ref[:, c0:c0 + COLS] = x1p_scr[:, c0:c0 + COLS] + _dot(f_scr[...], w2_scr[:, c0:c0 + COLS])

    def ffn_finish():
        if final_norm:
            o_ref[...] = _rmsnorm(o_ref[...], nfg_ref[...])

    def mixer_start():
        h_scr[...] = _rmsnorm(x_ref[...], n1g_ref[...]).astype(bf16)

    def in_proj(col0):
        return _dot(h_scr[...], win_scr[:, col0:col0 + COLS]) + bin_ref[:, col0:col0 + COLS]

    def glu_block(cb):
        c0 = cb * COLS
        a = in_proj(2 * D + c0) * _sigmoid(in_proj(3 * D + c0))
        for j in range(COLS // LANES):
            aext_scr[c0 // LANES + j, HALO:HALO + T, :] = a[:, j * LANES:(j + 1) * LANES]

    def conv_block(lb):
        tap0 = HALO - (CONV_KERNEL - 1)
        l0 = lb * LANES
        for rb in range(T // CONV_ROWS):
            r0 = rb * CONV_ROWS
            acc = jnp.zeros((CONV_ROWS, LANES), jnp.float32) + convb_ref[:, l0:l0 + LANES]
            for k in range(CONV_KERNEL):
                a_k = aext_scr[lb, pl.ds(r0 + tap0 + k, CONV_ROWS, stride=1), :]
                acc = acc + convw_ref[k:k + 1, l0:l0 + LANES] * a_k
            c_scr[r0:r0 + CONV_ROWS, l0:l0 + LANES] = acc

    def gelu_block(dst_scr, seg, cb):
        c0 = cb * COLS
        dst_scr[:, c0:c0 + COLS] = _gelu_tanh(in_proj(seg * D + c0))

    def gate_block(dst_scr, seg, cb):
        c0 = cb * COLS
        dst_scr[:, c0:c0 + COLS] = _sigmoid(in_proj(seg * D + c0))

    def halo_keep():
        aext_scr[:, 0:HALO, :] = aext_scr[:, T:T + HALO, :]

    def spatial_gating():
        vn_scr[...] = _layernorm(v_scr[...], lnag_ref[...], lnab_ref[...]).astype(bf16)
        row = lax.broadcasted_iota(jnp.int32, (CHUNK, CHUNK), 0)
        col = lax.broadcasted_iota(jnp.int32, (CHUNK, CHUNK), 1)
        causal = row >= col
        n_chunks = T // CHUNK
        for g in range(SGU_GROUPS):
            wg = jnp.where(causal, sguw_ref[g], 0.0).astype(bf16)
            c0 = g * LANES
            vg = jnp.concatenate(
                [vn_scr[n * CHUNK:(n + 1) * CHUNK, c0:c0 + LANES] for n in range(n_chunks)], axis=1)
            mixed = _dot(wg, vg)
            for n in range(n_chunks):
                r0 = n * CHUNK
                sg_scr[r0:r0 + CHUNK, c0:c0 + LANES] = (
                    u_scr[r0:r0 + CHUNK, c0:c0 + LANES]
                    * (mixed[:, n * LANES:(n + 1) * LANES] + sgub_ref[g])).astype(bf16)

    def conv_tail():
        cn = _layernorm(c_scr[...], lncg_ref[...], lncb_ref[...])
        cn_scr[...] = (cn * _sigmoid(cn)).astype(bf16)

    def merge():
        y_a = _dot(sg_scr[...], wpa_scr[...])
        y_b = _dot(cn_scr[...], wpb_scr[...]) + bpb_ref[...]
        m_scr[...] = (g0_scr[...] * y_a + g1_scr[...] * y_b).astype(bf16)

    def mixer_finish():
        x1 = x_ref[...] + _dot(m_scr[...], wout_scr[...])
        x1_scr[...] = x1
        h2_scr[...] = _rmsnorm(x1, n2g_ref[...]).astype(bf16)

    x1p_scr[...] = x1_scr[...]
    ffn_up(0)
    mixer_start()
    ffn_up(1)
    glu_block(0)
    ffn_up(2)
    glu_block(1)
    conv_block(0)
    ffn_up(3)
    conv_block(1)
    gelu_block(u_scr, 0, 0)
    conv_block(2)
    ffn_up(4)
    conv_block(3)
    gelu_block(u_scr, 0, 1)
    conv_block(4)
    ffn_up(5)
    conv_block(5)
    gelu_block(v_scr, 1, 0)
    conv_block(6)
    ffn_up(6)
    conv_block(7)
    gelu_block(v_scr, 1, 1)
    ffn_up(7)
    halo_keep()
    gate_block(g0_scr, 4, 0)
    spatial_gating()
    gate_block(g0_scr, 4, 1)
    conv_tail()
    ffn_down(0)
    gate_block(g1_scr, 5, 0)
    gate_block(g1_scr, 5, 1)
    merge()
    ffn_down(1)
    mixer_finish()
    ffn_finish()


def _resident(shape):
    zeros = (0,) * len(shape)
    return pl.BlockSpec(shape, lambda *_: zeros, pipeline_mode=pl.Buffered(1))


def _nbytes(shape, dtype):
    return math.prod(shape) * jnp.dtype(dtype).itemsize


def _vmem_limit(resident, streamed, scratch, temporaries):
    total = sum(_nbytes(*b) for b in resident) + 2 * sum(_nbytes(*b) for b in streamed)
    total += sum(_nbytes(*b) for b in scratch) + sum(_nbytes(*b) for b in temporaries)
    assert total <= VMEM_CAPACITY, total
    return total


def _block(x2d, layer, tiles_per_seq, final_norm, params, weights):
    M, D = x2d.shape
    F = weights[4].shape[2]
    T = TOKENS
    assert M % T == 0 and T % CHUNK == 0 and D % COLS == 0 and F % COLS == 0 and T % CONV_ROWS == 0
    n_tiles = M // T
    f32, bf16 = jnp.float32, jnp.bfloat16
    weight_buffers = [(w.shape[1:], bf16) for w in weights] + [((STAGE_SLOTS, STAGE_ROWS, STAGE_COLS), f32)]
    scratch = [
        ((T, D), bf16),
        ((T, D), f32),
        ((T, D), f32),
        ((T, D), bf16),
        ((T, D), bf16),
        ((D // LANES, HALO + T, LANES), f32),
        ((T, D), f32),
        ((T, D), bf16),
        ((T, D), f32),
        ((T, D), f32),
        ((T, D), bf16),
        ((T, D), f32),
        ((T, D), f32),
        ((T, D), bf16),
        ((T, F), bf16),
    ]
    limit = _vmem_limit(
        resident=[(p.shape, p.dtype) for p in params],
        streamed=[((T, D), f32), ((T, D), f32)],
        scratch=weight_buffers + scratch,
        temporaries=[((T, D), f32)] * 2)
    return pl.pallas_call(
        functools.partial(_block_kernel, layer=layer, final_norm=final_norm, n_tiles=n_tiles,
                          tiles_per_seq=tiles_per_seq),
        grid=(n_tiles + 1,),
        in_specs=[pl.BlockSpec((T, D), lambda i: (jnp.minimum(i, n_tiles - 1), 0))]
                 + [_resident(p.shape) for p in params]
                 + [pl.BlockSpec(memory_space=pl.ANY)] * len(weights),
        out_specs=pl.BlockSpec((T, D), lambda i: (jnp.maximum(i - 1, 0), 0)),
        out_shape=jax.ShapeDtypeStruct(x2d.shape, x2d.dtype),
        scratch_shapes=[pltpu.VMEM(shape, dtype) for shape, dtype in weight_buffers]
                       + [pltpu.SemaphoreType.DMA((STAGE_SLOTS,))]
                       + [pltpu.VMEM(shape, dtype) for shape, dtype in scratch],
        compiler_params=pltpu.CompilerParams(
            dimension_semantics=("arbitrary",), vmem_limit_bytes=limit),
        name="block",
    )(x2d, *params, *weights)


def kernel(x, norm1_g, w_in, b_in, sgu_ln_g, sgu_ln_b, sgu_w, sgu_b, w_proj_a, conv_w, conv_b,
           conv_ln_g, conv_ln_b, w_proj_b, b_proj_b, w_out, norm2_g, w_ff1, w_ff2, norm_f_g):
    B, S, D = x.shape
    depth = w_in.shape[0]
    row = lambda p: p.reshape(1, -1)
    x2d = x.reshape(B * S, D)
    for l in range(depth):
        sgub = jnp.broadcast_to(sgu_b[l][:, :, None], (SGU_GROUPS, CHUNK, LANES))
        params = (row(norm1_g[l]), row(b_in[l]), row(sgu_ln_g[l]), row(sgu_ln_b[l]), sgu_w[l], sgub,
                  conv_w[l], row(conv_b[l]), row(conv_ln_g[l]), row(conv_ln_b[l]), row(b_proj_b[l]),
                  row(norm2_g[l]), row(norm_f_g))
        x2d = _block(x2d, l, S // TOKENS, l == depth - 1, params,
                     (w_in, w_proj_a, w_proj_b, w_out, w_ff1, w_ff2))
    return x2d.reshape(B, S, D)
```

```python
import functools
import math

import jax
import jax.numpy as jnp
from jax import lax
from jax.experimental import pallas as pl
from jax.experimental.pallas import tpu as pltpu

CHUNK = 128
SGU_GROUPS = 8
CONV_KERNEL = 31
EPS = 1e-6

LANES = 128
HALO = 32
TOKENS = 256
COLS = 512
CONV_ROWS = 64
STAGE_ROWS, STAGE_COLS = 128, 1024
STAGE_SLOTS = 4
VMEM_CAPACITY = 64 * 1024 * 1024

_GELU_C = math.sqrt(2.0 / math.pi)


def _dot(a, b):
    return jnp.dot(a, b, preferred_element_type=jnp.float32)


def _sigmoid(x):
    return 0.5 * jnp.tanh(0.5 * x) + 0.5


def _gelu_tanh(x):
    return 0.5 * x * (1.0 + jnp.tanh(_GELU_C * (x + 0.044715 * (x * x * x))))


def _rmsnorm(x, g):
    return x * lax.rsqrt(jnp.mean(x * x, axis=-1, keepdims=True) + EPS) * g


def _layernorm(x, g, b):
    mu = jnp.mean(x, axis=-1, keepdims=True)
    xc = x - mu
    var = jnp.mean(xc * xc, axis=-1, keepdims=True)
    return xc * lax.rsqrt(var + EPS) * g + b


def _stage_weights(layer, pairs, stage_scr, sem):
    chunks = []
    for src, dst in pairs:
        rows, cols = dst.shape
        assert rows % STAGE_ROWS == 0 and cols % STAGE_COLS == 0
        for r0 in range(0, rows, STAGE_ROWS):
            for c0 in range(0, cols, STAGE_COLS):
                chunks.append((src, dst, r0, c0))

    def copy(j):
        src, _, r0, c0 = chunks[j]
        return pltpu.make_async_copy(
            src.at[layer, pl.ds(r0, STAGE_ROWS), pl.ds(c0, STAGE_COLS)],
            stage_scr.at[j % STAGE_SLOTS], sem.at[j % STAGE_SLOTS])

    lookahead = STAGE_SLOTS - 1
    for j in range(min(lookahead, len(chunks))):
        copy(j).start()
    for j, (_, dst, r0, c0) in enumerate(chunks):
        if j + lookahead < len(chunks):
            copy(j + lookahead).start()
        copy(j).wait()
        dst[r0:r0 + STAGE_ROWS, c0:c0 + STAGE_COLS] = stage_scr[j % STAGE_SLOTS].astype(jnp.bfloat16)


def _block_kernel(x_ref, n1g_ref, bin_ref, lnag_ref, lnab_ref, sguw_ref, sgub_ref,
                  convw_ref, convb_ref, lncg_ref, lncb_ref, bpb_ref, n2g_ref, nfg_ref,
                  win_hbm, wpa_hbm, wpb_hbm, wout_hbm, w1_hbm, w2_hbm,
                  o_ref,
                  win_scr, wpa_scr, wpb_scr, wout_scr, w1_scr, w2_scr, stage_scr, stage_sem,
                  h_scr, u_scr, v_scr, vn_scr, sg_scr, aext_scr, c_scr, cn_scr, g0_scr, g1_scr, m_scr,
                  x1_scr, x1p_scr, h2_scr, f_scr,
                  *, layer, final_norm, n_tiles, tiles_per_seq):
    T, D = h_scr.shape
    bf16 = jnp.bfloat16
    step = pl.program_id(0)
    assert D // COLS == 2 and D // LANES == 8 and f_scr.shape[1] // COLS == 8

    @pl.when(step == 0)
    def _():
        _stage_weights(layer, [(win_hbm, win_scr), (w1_hbm, w1_scr), (wpa_hbm, wpa_scr),
                               (wpb_hbm, wpb_scr), (wout_hbm, wout_scr), (w2_hbm, w2_scr)],
                       stage_scr, stage_sem)
        x1_scr[...] = jnp.zeros(x1_scr.shape, x1_scr.dtype)
        h2_scr[...] = jnp.zeros(h2_scr.shape, h2_scr.dtype)

    @pl.when(lax.rem(jnp.minimum(step, n_tiles - 1), tiles_per_seq) == 0)
    def _():
        aext_scr[:, 0:HALO, :] = jnp.zeros((D // LANES, HALO, LANES), jnp.float32)

    def ffn_up(cb):
        c0 = cb * COLS
        f = jnp.maximum(_dot(h2_scr[...], w1_scr[:, c0:c0 + COLS]), 0.0)
        f_scr[:, c0:c0 + COLS] = (f * f).astype(bf16)

    def ffn_down(cb):
        c0 = cb * COLS
        o_ref[:, c0:c0 + COLS] = x1p_scr[:, c0:c0 + COLS] + _dot(f_scr[...], w2_scr[:, c0:c0 + COLS])

    def ffn_finish():
        if final_norm:
            o_ref[...] = _rmsnorm(o_ref[...], nfg_ref[...])

    def mixer_start():
        h_scr[...] = _rmsnorm(x_ref[...], n1g_ref[...]).astype(bf16)

    def in_proj(col0):
        return _dot(h_scr[...], win_scr[:, col0:col0 + COLS]) + bin_ref[:, col0:col0 + COLS]

    def glu_block(cb):
        c0 = cb * COLS
        a = in_proj(2 * D + c0) * _sigmoid(in_proj(3 * D + c0))
        for j in range(COLS // LANES):
            aext_scr[c0 // LANES + j, HALO:HALO + T, :] = a[:, j * LANES:(j + 1) * LANES]

    def conv_block(lb):
        tap0 = HALO - (CONV_KERNEL - 1)
        l0 = lb * LANES
        for rb in range(T // CONV_ROWS):
            r0 = rb * CONV_ROWS
            acc = jnp.zeros((CONV_ROWS, LANES), jnp.float32) + convb_ref[:, l0:l0 + LANES]
            for k in range(CONV_KERNEL):
                a_k = aext_scr[lb, pl.ds(r0 + tap0 + k, CONV_ROWS, stride=1), :]
                acc = acc + convw_ref[k:k + 1, l0:l0 + LANES] * a_k
            c_scr[r0:r0 + CONV_ROWS, l0:l0 + LANES] = acc

    def gelu_block(dst_scr, seg, cb):
        c0 = cb * COLS
        dst_scr[:, c0:c0 + COLS] = _gelu_tanh(in_proj(seg * D + c0))

    def gate_block(dst_scr, seg, cb):
        c0 = cb * COLS
        dst_scr[:, c0:c0 + COLS] = _sigmoid(in_proj(seg * D + c0))

    def halo_keep():
        aext_scr[:, 0:HALO, :] = aext_scr[:, T:T + HALO, :]

    def spatial_gating():
        vn_scr[...] = _layernorm(v_scr[...], lnag_ref[...], lnab_ref[...]).astype(bf16)
        row = lax.broadcasted_iota(jnp.int32, (CHUNK, CHUNK), 0)
        col = lax.broadcasted_iota(jnp.int32, (CHUNK, CHUNK), 1)
        causal = row >= col
        n_chunks = T // CHUNK
        for g in range(SGU_GROUPS):
            wg = jnp.where(causal, sguw_ref[g], 0.0).astype(bf16)
            c0 = g * LANES
            vg = jnp.concatenate(
                [vn_scr[n * CHUNK:(n + 1) * CHUNK, c0:c0 + LANES] for n in range(n_chunks)], axis=1)
            mixed = _dot(wg, vg)
            for n in range(n_chunks):
                r0 = n * CHUNK
                sg_scr[r0:r0 + CHUNK, c0:c0 + LANES] = (
                    u_scr[r0:r0 + CHUNK, c0:c0 + LANES]
                    * (mixed[:, n * LANES:(n + 1) * LANES] + sgub_ref[g])).astype(bf16)

    def conv_tail():
        cn = _layernorm(c_scr[...], lncg_ref[...], lncb_ref[...])
        cn_scr[...] = (cn * _sigmoid(cn)).astype(bf16)

    def merge():
        y_a = _dot(sg_scr[...], wpa_scr[...])
        y_b = _dot(cn_scr[...], wpb_scr[...]) + bpb_ref[...]
        m_scr[...] = (g0_scr[...] * y_a + g1_scr[...] * y_b).astype(bf16)

    def mixer_finish():
        x1 = x_ref[...] + _dot(m_scr[...], wout_scr[...])
        x1_scr[...] = x1
        h2_scr[...] = _rmsnorm(x1, n2g_ref[...]).astype(bf16)

    x1p_scr[...] = x1_scr[...]
    ffn_up(0)
    mixer_start()
    ffn_up(1)
    glu_block(0)
    ffn_up(2)
    glu_block(1)
    conv_block(0)
    ffn_up(3)
    conv_block(1)
    ffn_up(4)
    conv_block(2)
    ffn_up(5)
    conv_block(3)
    ffn_up(6)
    conv_block(4)
    ffn_up(7)
    conv_block(5)
    ffn_down(0)
    conv_block(6)
    conv_block(7)
    ffn_down(1)
    halo_keep()
    ffn_finish()
    gelu_block(u_scr, 0, 0)
    gelu_block(u_scr, 0, 1)
    gelu_block(v_scr, 1, 0)
    conv_tail()
    gelu_block(v_scr, 1, 1)
    gate_block(g0_scr, 4, 0)
    spatial_gating()
    gate_block(g0_scr, 4, 1)
    gate_block(g1_scr, 5, 0)
    gate_block(g1_scr, 5, 1)
    merge()
    mixer_finish()


def _resident(shape):
    zeros = (0,) * len(shape)
    return pl.BlockSpec(shape, lambda *_: zeros, pipeline_mode=pl.Buffered(1))


def _nbytes(shape, dtype):
    return math.prod(shape) * jnp.dtype(dtype).itemsize


def _vmem_limit(resident, streamed, scratch, temporaries):
    total = sum(_nbytes(*b) for b in resident) + 2 * sum(_nbytes(*b) for b in streamed)
    total += sum(_nbytes(*b) for b in scratch) + sum(_nbytes(*b) for b in temporaries)
    assert total <= VMEM_CAPACITY, total
    return total


def _block(x2d, layer, tiles_per_seq, final_norm, params, weights):
    M, D = x2d.shape
    F = weights[4].shape[2]
    T = TOKENS
    assert M % T == 0 and T % CHUNK == 0 and D % COLS == 0 and F % COLS == 0 and T % CONV_ROWS == 0
    n_tiles = M // T
    f32, bf16 = jnp.float32, jnp.bfloat16
    weight_buffers = [(w.shape[1:], bf16) for w in weights] + [((STAGE_SLOTS, STAGE_ROWS, STAGE_COLS), f32)]
    scratch = [
        ((T, D), bf16),
        ((T, D), f32),
        ((T, D), f32),
        ((T, D), bf16),
        ((T, D), bf16),
        ((D // LANES, HALO + T, LANES), f32),
        ((T, D), f32),
        ((T, D), bf16),
        ((T, D), f32),
        ((T, D), f32),
        ((T, D), bf16),
        ((T, D), f32),
        ((T, D), f32),
        ((T, D), bf16),
        ((T, F), bf16),
    ]
    limit = _vmem_limit(
        resident=[(p.shape, p.dtype) for p in params],
        streamed=[((T, D), f32), ((T, D), f32)],
        scratch=weight_buffers + scratch,
        temporaries=[((T, D), f32)] * 2)
    return pl.pallas_call(
        functools.partial(_block_kernel, layer=layer, final_norm=final_norm, n_tiles=n_tiles,
                          tiles_per_seq=tiles_per_seq),
        grid=(n_tiles + 1,),
        in_specs=[pl.BlockSpec((T, D), lambda i: (jnp.minimum(i, n_tiles - 1), 0))]
                 + [_resident(p.shape) for p in params]
                 + [pl.BlockSpec(memory_space=pl.ANY)] * len(weights),
        out_specs=pl.BlockSpec((T, D), lambda i: (jnp.maximum(i - 1, 0), 0)),
        out_shape=jax.ShapeDtypeStruct(x2d.shape, x2d.dtype),
        scratch_shapes=[pltpu.VMEM(shape, dtype) for shape, dtype in weight_buffers]
                       + [pltpu.SemaphoreType.DMA((STAGE_SLOTS,))]
                       + [pltpu.VMEM(shape, dtype) for shape, dtype in scratch],
        compiler_params=pltpu.CompilerParams(
            dimension_semantics=("arbitrary",), vmem_limit_bytes=limit),
        name="block",
    )(x2d, *params, *weights)


def kernel(x, norm1_g, w_in, b_in, sgu_ln_g, sgu_ln_b, sgu_w, sgu_b, w_proj_a, conv_w, conv_b,
           conv_ln_g, conv_ln_b, w_proj_b, b_proj_b, w_out, norm2_g, w_ff1, w_ff2, norm_f_g):
    B, S, D = x.shape
    depth = w_in.shape[0]
    row = lambda p: p.reshape(1, -1)
    x2d = x.reshape(B * S, D)
    for l in range(depth):
        sgub = jnp.broadcast_to(sgu_b[l][:, :, None], (SGU_GROUPS, CHUNK, LANES))
        params = (row(norm1_g[l]), row(b_in[l]), row(sgu_ln_g[l]), row(sgu_ln_b[l]), sgu_w[l], sgub,
                  conv_w[l], row(conv_b[l]), row(conv_ln_g[l]), row(conv_ln_b[l]), row(b_proj_b[l]),
                  row(norm2_g[l]), row(norm_f_g))
        x2d = _block(x2d, l, S // TOKENS, l == depth - 1, params,
                     (w_in, w_proj_a, w_proj_b, w_out, w_ff1, w_ff2))
    return x2d.reshape(B, S, D)
```

```python
import functools
import math

import jax
import jax.numpy as jnp
from jax import lax
from jax.experimental import pallas as pl
from jax.experimental.pallas import tpu as pltpu

CHUNK = 128
SGU_GROUPS = 8
CONV_KERNEL = 31
EPS = 1e-6

LANES = 128
HALO = 32
TOKENS = 256
COLS = 512
CONV_ROWS = 16
STAGE_ROWS, STAGE_COLS = 128, 1024
STAGE_SLOTS = 4
VMEM_CAPACITY = 64 * 1024 * 1024

_GELU_C = math.sqrt(2.0 / math.pi)


def _dot(a, b):
    return jnp.dot(a, b, preferred_element_type=jnp.float32)


def _sigmoid(x):
    return 0.5 * jnp.tanh(0.5 * x) + 0.5


def _gelu_tanh(x):
    return 0.5 * x * (1.0 + jnp.tanh(_GELU_C * (x + 0.044715 * (x * x * x))))


def _rmsnorm(x, g):
    return x * lax.rsqrt(jnp.mean(x * x, axis=-1, keepdims=True) + EPS) * g


def _layernorm(x, g, b):
    mu = jnp.mean(x, axis=-1, keepdims=True)
    xc = x - mu
    var = jnp.mean(xc * xc, axis=-1, keepdims=True)
    return xc * lax.rsqrt(var + EPS) * g + b


def _stage_weights(layer, pairs, stage_scr, sem):
    chunks = []
    for src, dst in pairs:
        rows, cols = dst.shape
        assert rows % STAGE_ROWS == 0 and cols % STAGE_COLS == 0
        for r0 in range(0, rows, STAGE_ROWS):
            for c0 in range(0, cols, STAGE_COLS):
                chunks.append((src, dst, r0, c0))

    def copy(j):
        src, _, r0, c0 = chunks[j]
        return pltpu.make_async_copy(
            src.at[layer, pl.ds(r0, STAGE_ROWS), pl.ds(c0, STAGE_COLS)],
            stage_scr.at[j % STAGE_SLOTS], sem.at[j % STAGE_SLOTS])

    lookahead = STAGE_SLOTS - 1
    for j in range(min(lookahead, len(chunks))):
        copy(j).start()
    for j, (_, dst, r0, c0) in enumerate(chunks):
        if j + lookahead < len(chunks):
            copy(j + lookahead).start()
        copy(j).wait()
        dst[r0:r0 + STAGE_ROWS, c0:c0 + STAGE_COLS] = stage_scr[j % STAGE_SLOTS].astype(jnp.bfloat16)


def _block_kernel(x_ref, n1g_ref, bin_ref, lnag_ref, lnab_ref, sguw_ref, sgub_ref,
                  convw_ref, convb_ref, lncg_ref, lncb_ref, bpb_ref, n2g_ref, nfg_ref,
                  win_hbm, wpa_hbm, wpb_hbm, wout_hbm, w1_hbm, w2_hbm,
                  o_ref,
                  win_scr, wpa_scr, wpb_scr, wout_scr, w1_scr, w2_scr, stage_scr, stage_sem,
                  h_scr, u_scr, v_scr, vn_scr, sg_scr, aext_scr, c_scr, cn_scr, g0_scr, g1_scr, m_scr,
                  x1_scr, h2_scr, f_scr,
                  *, layer, final_norm, n_tiles, tiles_per_seq):
    T, D = h_scr.shape
    bf16 = jnp.bfloat16
    step = pl.program_id(0)

    @pl.when(step == 0)
    def _():
        _stage_weights(layer, [(win_hbm, win_scr), (w1_hbm, w1_scr), (wpa_hbm, wpa_scr),
                               (wpb_hbm, wpb_scr), (wout_hbm, wout_scr), (w2_hbm, w2_scr)],
                       stage_scr, stage_sem)
        x1_scr[...] = jnp.zeros(x1_scr.shape, x1_scr.dtype)
        h2_scr[...] = jnp.zeros(h2_scr.shape, h2_scr.dtype)

    @pl.when(lax.rem(jnp.minimum(step, n_tiles - 1), tiles_per_seq) == 0)
    def _():
        aext_scr[:, 0:HALO, :] = jnp.zeros((D // LANES, HALO, LANES), jnp.float32)

    def ffn_up(cb):
        c0 = cb * COLS
        f = jnp.maximum(_dot(h2_scr[...], w1_scr[:, c0:c0 + COLS]), 0.0)
        f_scr[:, c0:c0 + COLS] = (f * f).astype(bf16)

    def ffn_down(cb):
        c0 = cb * COLS
        o_ref[:, c0:c0 + COLS] = x1_scr[:, c0:c0 + COLS] + _dot(f_scr[...], w2_scr[:, c0:c0 + COLS])

    def ffn_finish():
        if final_norm:
            o_ref[...] = _rmsnorm(o_ref[...], nfg_ref[...])

    def mixer_start():
        h_scr[...] = _rmsnorm(x_ref[...], n1g_ref[...]).astype(bf16)

    def in_proj(col0):
        return _dot(h_scr[...], win_scr[:, col0:col0 + COLS]) + bin_ref[:, col0:col0 + COLS]

    def glu_block(cb):
        c0 = cb * COLS
        a = in_proj(2 * D + c0) * _sigmoid(in_proj(3 * D + c0))
        for j in range(COLS // LANES):
            aext_scr[c0 // LANES + j, HALO:HALO + T, :] = a[:, j * LANES:(j + 1) * LANES]

    def conv_block(lb):
        tap0 = HALO - (CONV_KERNEL - 1)
        l0 = lb * LANES
        for rb in range(T // CONV_ROWS):
            r0 = rb * CONV_ROWS
            acc = jnp.zeros((CONV_ROWS, LANES), jnp.float32) + convb_ref[:, l0:l0 + LANES]
            for k in range(CONV_KERNEL):
                a_k = aext_scr[lb, pl.ds(r0 + tap0 + k, CONV_ROWS, stride=1), :]
                acc = acc + convw_ref[k:k + 1, l0:l0 + LANES] * a_k
            c_scr[r0:r0 + CONV_ROWS, l0:l0 + LANES] = acc

    def gelu_block(dst_scr, seg, cb):
        c0 = cb * COLS
        dst_scr[:, c0:c0 + COLS] = _gelu_tanh(in_proj(seg * D + c0))

    def gate_block(dst_scr, seg, cb):
        c0 = cb * COLS
        dst_scr[:, c0:c0 + COLS] = _sigmoid(in_proj(seg * D + c0))

    def halo_keep():
        aext_scr[:, 0:HALO, :] = aext_scr[:, T:T + HALO, :]

    def spatial_gating():
        vn_scr[...] = _layernorm(v_scr[...], lnag_ref[...], lnab_ref[...]).astype(bf16)
        row = lax.broadcasted_iota(jnp.int32, (CHUNK, CHUNK), 0)
        col = lax.broadcasted_iota(jnp.int32, (CHUNK, CHUNK), 1)
        causal = row >= col
        n_chunks = T // CHUNK
        for g in range(SGU_GROUPS):
            wg = jnp.where(causal, sguw_ref[g], 0.0).astype(bf16)
            c0 = g * LANES
            vg = jnp.concatenate(
                [vn_scr[n * CHUNK:(n + 1) * CHUNK, c0:c0 + LANES] for n in range(n_chunks)], axis=1)
            mixed = _dot(wg, vg)
            for n in range(n_chunks):
                r0 = n * CHUNK
                sg_scr[r0:r0 + CHUNK, c0:c0 + LANES] = (
                    u_scr[r0:r0 + CHUNK, c0:c0 + LANES]
                    * (mixed[:, n * LANES:(n + 1) * LANES] + sgub_ref[g])).astype(bf16)

    def conv_tail():
        cn = _layernorm(c_scr[...], lncg_ref[...], lncb_ref[...])
        cn_scr[...] = (cn * _sigmoid(cn)).astype(bf16)

    def merge():
        y_a = _dot(sg_scr[...], wpa_scr[...])
        y_b = _dot(cn_scr[...], wpb_scr[...]) + bpb_ref[...]
        m_scr[...] = (g0_scr[...] * y_a + g1_scr[...] * y_b).astype(bf16)

    def mixer_finish():
        x1 = x_ref[...] + _dot(m_scr[...], wout_scr[...])
        x1_scr[...] = x1
        h2_scr[...] = _rmsnorm(x1, n2g_ref[...]).astype(bf16)

    n_in = D // COLS
    ffn_work = ([functools.partial(ffn_up, cb) for cb in range(f_scr.shape[1] // COLS)]
                + [functools.partial(ffn_down, cb) for cb in range(n_in)])

    def next_ffn():
        if ffn_work:
            ffn_work.pop(0)()

    next_ffn()
    mixer_start()
    for cb in range(n_in):
        next_ffn()
        glu_block(cb)
    for lb in range(D // LANES):
        conv_block(lb)
        next_ffn()
    while ffn_work:
        next_ffn()
    halo_keep()
    ffn_finish()
    for cb in range(n_in):
        gelu_block(u_scr, 0, cb)
    for cb in range(n_in):
        if cb == n_in - 1:
            conv_tail()
        gelu_block(v_scr, 1, cb)
    gate_block(g0_scr, 4, 0)
    spatial_gating()
    for cb in range(1, n_in):
        gate_block(g0_scr, 4, cb)
    for cb in range(n_in):
        gate_block(g1_scr, 5, cb)
    merge()
    mixer_finish()


def _resident(shape):
    zeros = (0,) * len(shape)
    return pl.BlockSpec(shape, lambda *_: zeros, pipeline_mode=pl.Buffered(1))


def _nbytes(shape, dtype):
    return math.prod(shape) * jnp.dtype(dtype).itemsize


def _vmem_limit(resident, streamed, scratch, temporaries):
    total = sum(_nbytes(*b) for b in resident) + 2 * sum(_nbytes(*b) for b in streamed)
    total += sum(_nbytes(*b) for b in scratch) + sum(_nbytes(*b) for b in temporaries)
    assert total <= VMEM_CAPACITY, total
    return total


def _block(x2d, layer, tiles_per_seq, final_norm, params, weights):
    M, D = x2d.shape
    F = weights[4].shape[2]
    T = TOKENS
    assert M % T == 0 and T % CHUNK == 0 and D % COLS == 0 and F % COLS == 0 and T % CONV_ROWS == 0
    n_tiles = M // T
    f32, bf16 = jnp.float32, jnp.bfloat16
    weight_buffers = [(w.shape[1:], bf16) for w in weights] + [((STAGE_SLOTS, STAGE_ROWS, STAGE_COLS), f32)]
    scratch = [
        ((T, D), bf16),
        ((T, D), f32),
        ((T, D), f32),
        ((T, D), bf16),
        ((T, D), bf16),
        ((D // LANES, HALO + T, LANES), f32),
        ((T, D), f32),
        ((T, D), bf16),
        ((T, D), f32),
        ((T, D), f32),
        ((T, D), bf16),
        ((T, D), f32),
        ((T, D), bf16),
        ((T, F), bf16),
    ]
    limit = _vmem_limit(
        resident=[(p.shape, p.dtype) for p in params],
        streamed=[((T, D), f32), ((T, D), f32)],
        scratch=weight_buffers + scratch,
        temporaries=[((T, D), f32)] * 2)
    return pl.pallas_call(
        functools.partial(_block_kernel, layer=layer, final_norm=final_norm, n_tiles=n_tiles,
                          tiles_per_seq=tiles_per_seq),
        grid=(n_tiles + 1,),
        in_specs=[pl.BlockSpec((T, D), lambda i: (jnp.minimum(i, n_tiles - 1), 0))]
                 + [_resident(p.shape) for p in params]
                 + [pl.BlockSpec(memory_space=pl.ANY)] * len(weights),
        out_specs=pl.BlockSpec((T, D), lambda i: (jnp.maximum(i - 1, 0), 0)),
        out_shape=jax.ShapeDtypeStruct(x2d.shape, x2d.dtype),
        scratch_shapes=[pltpu.VMEM(shape, dtype) for shape, dtype in weight_buffers]
                       + [pltpu.SemaphoreType.DMA((STAGE_SLOTS,))]
                       + [pltpu.VMEM(shape, dtype) for shape, dtype in scratch],
        compiler_params=pltpu.CompilerParams(
            dimension_semantics=("arbitrary",), vmem_limit_bytes=limit),
        name="block",
    )(x2d, *params, *weights)


def kernel(x, norm1_g, w_in, b_in, sgu_ln_g, sgu_ln_b, sgu_w, sgu_b, w_proj_a, conv_w, conv_b,
           conv_ln_g, conv_ln_b, w_proj_b, b_proj_b, w_out, norm2_g, w_ff1, w_ff2, norm_f_g):
    B, S, D = x.shape
    depth = w_in.shape[0]
    row = lambda p: p.reshape(1, -1)
    x2d = x.reshape(B * S, D)
    for l in range(depth):
        sgub = jnp.broadcast_to(sgu_b[l][:, :, None], (SGU_GROUPS, CHUNK, LANES))
        params = (row(norm1_g[l]), row(b_in[l]), row(sgu_ln_g[l]), row(sgu_ln_b[l]), sgu_w[l], sgub,
                  conv_w[l], row(conv_b[l]), row(conv_ln_g[l]), row(conv_ln_b[l]), row(b_proj_b[l]),
                  row(norm2_g[l]), row(norm_f_g))
        x2d = _block(x2d, l, S // TOKENS, l == depth - 1, params,
                     (w_in, w_proj_a, w_proj_b, w_out, w_ff1, w_ff2))
    return x2d.reshape(B, S, D)
```

```python
import functools
import math

import jax
import jax.numpy as jnp
from jax import lax
from jax.experimental import pallas as pl
from jax.experimental.pallas import tpu as pltpu

CHUNK = 128
SGU_GROUPS = 8
CONV_KERNEL = 31
EPS = 1e-6

LANES = 128
HALO = 32
TOKENS = 256
COLS = 512
CONV_ROWS = 32
STAGE_ROWS, STAGE_COLS = 128, 1024
STAGE_SLOTS = 4
VMEM_CAPACITY = 64 * 1024 * 1024

_GELU_C = math.sqrt(2.0 / math.pi)


def _dot(a, b):
    return jnp.dot(a, b, preferred_element_type=jnp.float32)


def _sigmoid(x):
    return 0.5 * jnp.tanh(0.5 * x) + 0.5


def _gelu_tanh(x):
    return 0.5 * x * (1.0 + jnp.tanh(_GELU_C * (x + 0.044715 * (x * x * x))))


def _rmsnorm(x, g):
    return x * lax.rsqrt(jnp.mean(x * x, axis=-1, keepdims=True) + EPS) * g


def _layernorm(x, g, b):
    mu = jnp.mean(x, axis=-1, keepdims=True)
    xc = x - mu
    var = jnp.mean(xc * xc, axis=-1, keepdims=True)
    return xc * lax.rsqrt(var + EPS) * g + b


def _stage_weights(layer, pairs, stage_scr, sem):
    chunks = []
    for src, dst in pairs:
        rows, cols = dst.shape
        assert rows % STAGE_ROWS == 0 and cols % STAGE_COLS == 0
        for r0 in range(0, rows, STAGE_ROWS):
            for c0 in range(0, cols, STAGE_COLS):
                chunks.append((src, dst, r0, c0))

    def copy(j):
        src, _, r0, c0 = chunks[j]
        return pltpu.make_async_copy(
            src.at[layer, pl.ds(r0, STAGE_ROWS), pl.ds(c0, STAGE_COLS)],
            stage_scr.at[j % STAGE_SLOTS], sem.at[j % STAGE_SLOTS])

    lookahead = STAGE_SLOTS - 1
    for j in range(min(lookahead, len(chunks))):
        copy(j).start()
    for j, (_, dst, r0, c0) in enumerate(chunks):
        if j + lookahead < len(chunks):
            copy(j + lookahead).start()
        copy(j).wait()
        dst[r0:r0 + STAGE_ROWS, c0:c0 + STAGE_COLS] = stage_scr[j % STAGE_SLOTS].astype(jnp.bfloat16)


def _block_kernel(x_ref, n1g_ref, bin_ref, lnag_ref, lnab_ref, sguw_ref, sgub_ref,
                  convw_ref, convb_ref, lncg_ref, lncb_ref, bpb_ref, n2g_ref, nfg_ref,
                  win_hbm, wpa_hbm, wpb_hbm, wout_hbm, w1_hbm, w2_hbm,
                  o_ref,
                  win_scr, wpa_scr, wpb_scr, wout_scr, w1_scr, w2_scr, stage_scr, stage_sem,
                  h_scr, u_scr, v_scr, vn_scr, sg_scr, aext_scr, c_scr, cn_scr, g0_scr, g1_scr, m_scr,
                  x1_scr, h2_scr, f_scr,
                  *, layer, final_norm, n_tiles, tiles_per_seq):
    T, D = h_scr.shape
    bf16 = jnp.bfloat16
    step = pl.program_id(0)

    @pl.when(step == 0)
    def _():
        _stage_weights(layer, [(win_hbm, win_scr), (w1_hbm, w1_scr), (wpa_hbm, wpa_scr),
                               (wpb_hbm, wpb_scr), (wout_hbm, wout_scr), (w2_hbm, w2_scr)],
                       stage_scr, stage_sem)
        x1_scr[...] = jnp.zeros(x1_scr.shape, x1_scr.dtype)
        h2_scr[...] = jnp.zeros(h2_scr.shape, h2_scr.dtype)

    @pl.when(lax.rem(jnp.minimum(step, n_tiles - 1), tiles_per_seq) == 0)
    def _():
        aext_scr[:, 0:HALO, :] = jnp.zeros((D // LANES, HALO, LANES), jnp.float32)

    def ffn_up(cb):
        c0 = cb * COLS
        f = jnp.maximum(_dot(h2_scr[...], w1_scr[:, c0:c0 + COLS]), 0.0)
        f_scr[:, c0:c0 + COLS] = (f * f).astype(bf16)

    def ffn_down(cb):
        c0 = cb * COLS
        o_ref[:, c0:c0 + COLS] = x1_scr[:, c0:c0 + COLS] + _dot(f_scr[...], w2_scr[:, c0:c0 + COLS])

    def ffn_finish():
        if final_norm:
            o_ref[...] = _rmsnorm(o_ref[...], nfg_ref[...])

    def mixer_start():
        h_scr[...] = _rmsnorm(x_ref[...], n1g_ref[...]).astype(bf16)

    def in_proj(col0):
        return _dot(h_scr[...], win_scr[:, col0:col0 + COLS]) + bin_ref[:, col0:col0 + COLS]

    def glu_block(cb):
        c0 = cb * COLS
        a = in_proj(2 * D + c0) * _sigmoid(in_proj(3 * D + c0))
        for j in range(COLS // LANES):
            aext_scr[c0 // LANES + j, HALO:HALO + T, :] = a[:, j * LANES:(j + 1) * LANES]

    def conv_block(lb):
        tap0 = HALO - (CONV_KERNEL - 1)
        l0 = lb * LANES
        for rb in range(T // CONV_ROWS):
            r0 = rb * CONV_ROWS
            acc = jnp.zeros((CONV_ROWS, LANES), jnp.float32) + convb_ref[:, l0:l0 + LANES]
            for k in range(CONV_KERNEL):
                a_k = aext_scr[lb, pl.ds(r0 + tap0 + k, CONV_ROWS, stride=1), :]
                acc = acc + convw_ref[k:k + 1, l0:l0 + LANES] * a_k
            c_scr[r0:r0 + CONV_ROWS, l0:l0 + LANES] = acc

    def gelu_block(dst_scr, seg, cb):
        c0 = cb * COLS
        dst_scr[:, c0:c0 + COLS] = _gelu_tanh(in_proj(seg * D + c0))

    def gate_block(dst_scr, seg, cb):
        c0 = cb * COLS
        dst_scr[:, c0:c0 + COLS] = _sigmoid(in_proj(seg * D + c0))

    def halo_keep():
        aext_scr[:, 0:HALO, :] = aext_scr[:, T:T + HALO, :]

    def spatial_gating():
        vn_scr[...] = _layernorm(v_scr[...], lnag_ref[...], lnab_ref[...]).astype(bf16)
        row = lax.broadcasted_iota(jnp.int32, (CHUNK, CHUNK), 0)
        col = lax.broadcasted_iota(jnp.int32, (CHUNK, CHUNK), 1)
        causal = row >= col
        n_chunks = T // CHUNK
        for g in range(SGU_GROUPS):
            wg = jnp.where(causal, sguw_ref[g], 0.0).astype(bf16)
            c0 = g * LANES
            vg = jnp.concatenate(
                [vn_scr[n * CHUNK:(n + 1) * CHUNK, c0:c0 + LANES] for n in range(n_chunks)], axis=1)
            mixed = _dot(wg, vg)
            for n in range(n_chunks):
                r0 = n * CHUNK
                sg_scr[r0:r0 + CHUNK, c0:c0 + LANES] = (
                    u_scr[r0:r0 + CHUNK, c0:c0 + LANES]
                    * (mixed[:, n * LANES:(n + 1) * LANES] + sgub_ref[g])).astype(bf16)

    def conv_tail():
        cn = _layernorm(c_scr[...], lncg_ref[...], lncb_ref[...])
        cn_scr[...] = (cn * _sigmoid(cn)).astype(bf16)

    def merge():
        y_a = _dot(sg_scr[...], wpa_scr[...])
        y_b = _dot(cn_scr[...], wpb_scr[...]) + bpb_ref[...]
        m_scr[...] = (g0_scr[...] * y_a + g1_scr[...] * y_b).astype(bf16)

    def mixer_finish():
        x1 = x_ref[...] + _dot(m_scr[...], wout_scr[...])
        x1_scr[...] = x1
        h2_scr[...] = _rmsnorm(x1, n2g_ref[...]).astype(bf16)

    n_in = D // COLS
    ffn_work = ([functools.partial(ffn_up, cb) for cb in range(f_scr.shape[1] // COLS)]
                + [functools.partial(ffn_down, cb) for cb in range(n_in)])

    def next_ffn():
        if ffn_work:
            ffn_work.pop(0)()

    next_ffn()
    mixer_start()
    for cb in range(n_in):
        next_ffn()
        glu_block(cb)
    for lb in range(D // LANES):
        conv_block(lb)
        next_ffn()
    while ffn_work:
        next_ffn()
    halo_keep()
    ffn_finish()
    for cb in range(n_in):
        gelu_block(u_scr, 0, cb)
    for cb in range(n_in):
        if cb == n_in - 1:
            conv_tail()
        gelu_block(v_scr, 1, cb)
    gate_block(g0_scr, 4, 0)
    spatial_gating()
    for cb in range(1, n_in):
        gate_block(g0_scr, 4, cb)
    for cb in range(n_in):
        gate_block(g1_scr, 5, cb)
    merge()
    mixer_finish()


def _resident(shape):
    zeros = (0,) * len(shape)
    return pl.BlockSpec(shape, lambda *_: zeros, pipeline_mode=pl.Buffered(1))


def _nbytes(shape, dtype):
    return math.prod(shape) * jnp.dtype(dtype).itemsize


def _vmem_limit(resident, streamed, scratch, temporaries):
    total = sum(_nbytes(*b) for b in resident) + 2 * sum(_nbytes(*b) for b in streamed)
    total += sum(_nbytes(*b) for b in scratch) + sum(_nbytes(*b) for b in temporaries)
    assert total <= VMEM_CAPACITY, total
    return total


def _block(x2d, layer, tiles_per_seq, final_norm, params, weights):
    M, D = x2d.shape
    F = weights[4].shape[2]
    T = TOKENS
    assert M % T == 0 and T % CHUNK == 0 and D % COLS == 0 and F % COLS == 0 and T % CONV_ROWS == 0
    n_tiles = M // T
    f32, bf16 = jnp.float32, jnp.bfloat16
    weight_buffers = [(w.shape[1:], bf16) for w in weights] + [((STAGE_SLOTS, STAGE_ROWS, STAGE_COLS), f32)]
    scratch = [
        ((T, D), bf16),
        ((T, D), f32),
        ((T, D), f32),
        ((T, D), bf16),
        ((T, D), bf16),
        ((D // LANES, HALO + T, LANES), f32),
        ((T, D), f32),
        ((T, D), bf16),
        ((T, D), f32),
        ((T, D), f32),
        ((T, D), bf16),
        ((T, D), f32),
        ((T, D), bf16),
        ((T, F), bf16),
    ]
    limit = _vmem_limit(
        resident=[(p.shape, p.dtype) for p in params],
        streamed=[((T, D), f32), ((T, D), f32)],
        scratch=weight_buffers + scratch,
        temporaries=[((T, D), f32)] * 2)
    return pl.pallas_call(
        functools.partial(_block_kernel, layer=layer, final_norm=final_norm, n_tiles=n_tiles,
                          tiles_per_seq=tiles_per_seq),
        grid=(n_tiles + 1,),
        in_specs=[pl.BlockSpec((T, D), lambda i: (jnp.minimum(i, n_tiles - 1), 0))]
                 + [_resident(p.shape) for p in params]
                 + [pl.BlockSpec(memory_space=pl.ANY)] * len(weights),
        out_specs=pl.BlockSpec((T, D), lambda i: (jnp.maximum(i - 1, 0), 0)),
        out_shape=jax.ShapeDtypeStruct(x2d.shape, x2d.dtype),
        scratch_shapes=[pltpu.VMEM(shape, dtype) for shape, dtype in weight_buffers]
                       + [pltpu.SemaphoreType.DMA((STAGE_SLOTS,))]
                       + [pltpu.VMEM(shape, dtype) for shape, dtype in scratch],
        compiler_params=pltpu.CompilerParams(
            dimension_semantics=("arbitrary",), vmem_limit_bytes=limit),
        name="block",
    )(x2d, *params, *weights)


def kernel(x, norm1_g, w_in, b_in, sgu_ln_g, sgu_ln_b, sgu_w, sgu_b, w_proj_a, conv_w, conv_b,
           conv_ln_g, conv_ln_b, w_proj_b, b_proj_b, w_out, norm2_g, w_ff1, w_ff2, norm_f_g):
    B, S, D = x.shape
    depth = w_in.shape[0]
    row = lambda p: p.reshape(1, -1)
    x2d = x.reshape(B * S, D)
    for l in range(depth):
        sgub = jnp.broadcast_to(sgu_b[l][:, :, None], (SGU_GROUPS, CHUNK, LANES))
        params = (row(norm1_g[l]), row(b_in[l]), row(sgu_ln_g[l]), row(sgu_ln_b[l]), sgu_w[l], sgub,
                  conv_w[l], row(conv_b[l]), row(conv_ln_g[l]), row(conv_ln_b[l]), row(b_proj_b[l]),
                  row(norm2_g[l]), row(norm_f_g))
        x2d = _block(x2d, l, S // TOKENS, l == depth - 1, params,
                     (w_in, w_proj_a, w_proj_b, w_out, w_ff1, w_ff2))
    return x2d.reshape(B, S, D)
```

```python
import functools
import math

import jax
import jax.numpy as jnp
from jax import lax
from jax.experimental import pallas as pl
from jax.experimental.pallas import tpu as pltpu

CHUNK = 128
SGU_GROUPS = 8
CONV_KERNEL = 31
EPS = 1e-6

LANES = 128
HALO = 32
TOKENS = 256
COLS = 512
CONV_ROWS = 16
STAGE_ROWS, STAGE_COLS = 128, 1024
STAGE_SLOTS = 4
VMEM_CAPACITY = 64 * 1024 * 1024

_GELU_C = math.sqrt(2.0 / math.pi)


def _dot(a, b):
    return jnp.dot(a, b, preferred_element_type=jnp.float32)


def _sigmoid(x):
    return 0.5 * jnp.tanh(0.5 * x) + 0.5


def _gelu_tanh(x):
    half = 0.5 * x
    return half + half * jnp.tanh(x * (_GELU_C + (_GELU_C * 0.044715) * (x * x)))


def _rmsnorm(x, g):
    return x * lax.rsqrt(jnp.mean(x * x, axis=-1, keepdims=True) + EPS) * g


def _layernorm(x, g, b):
    mu = jnp.mean(x, axis=-1, keepdims=True)
    xc = x - mu
    var = jnp.mean(xc * xc, axis=-1, keepdims=True)
    return xc * lax.rsqrt(var + EPS) * g + b


def _stage_weights(layer, pairs, stage_scr, sem):
    chunks = []
    for src, dst in pairs:
        rows, cols = dst.shape
        assert rows % STAGE_ROWS == 0 and cols % STAGE_COLS == 0
        for r0 in range(0, rows, STAGE_ROWS):
            for c0 in range(0, cols, STAGE_COLS):
                chunks.append((src, dst, r0, c0))

    def copy(j):
        src, _, r0, c0 = chunks[j]
        return pltpu.make_async_copy(
            src.at[layer, pl.ds(r0, STAGE_ROWS), pl.ds(c0, STAGE_COLS)],
            stage_scr.at[j % STAGE_SLOTS], sem.at[j % STAGE_SLOTS])

    lookahead = STAGE_SLOTS - 1
    for j in range(min(lookahead, len(chunks))):
        copy(j).start()
    for j, (_, dst, r0, c0) in enumerate(chunks):
        if j + lookahead < len(chunks):
            copy(j + lookahead).start()
        copy(j).wait()
        dst[r0:r0 + STAGE_ROWS, c0:c0 + STAGE_COLS] = stage_scr[j % STAGE_SLOTS].astype(jnp.bfloat16)


def _block_kernel(x_ref, n1g_ref, bin_ref, lnag_ref, lnab_ref, sguw_ref, sgub_ref,
                  convw_ref, convb_ref, lncg_ref, lncb_ref, bpb_ref, n2g_ref, nfg_ref,
                  win_hbm, wpa_hbm, wpb_hbm, wout_hbm, w1_hbm, w2_hbm,
                  o_ref,
                  win_scr, wpa_scr, wpb_scr, wout_scr, w1_scr, w2_scr, stage_scr, stage_sem,
                  h_scr, u_scr, v_scr, vn_scr, sg_scr, aext_scr, c_scr, cn_scr, g0_scr, g1_scr, m_scr,
                  x1_scr, h2_scr, f_scr, sguwm_scr,
                  *, layer, final_norm, n_tiles, tiles_per_seq):
    T, D = h_scr.shape
    bf16 = jnp.bfloat16
    step = pl.program_id(0)

    @pl.when(step == 0)
    def _():
        _stage_weights(layer, [(win_hbm, win_scr), (w1_hbm, w1_scr), (wpa_hbm, wpa_scr),
                               (wpb_hbm, wpb_scr), (wout_hbm, wout_scr), (w2_hbm, w2_scr)],
                       stage_scr, stage_sem)
        x1_scr[...] = jnp.zeros(x1_scr.shape, x1_scr.dtype)
        h2_scr[...] = jnp.zeros(h2_scr.shape, h2_scr.dtype)
        row = lax.broadcasted_iota(jnp.int32, (CHUNK, CHUNK), 0)
        col = lax.broadcasted_iota(jnp.int32, (CHUNK, CHUNK), 1)
        for g in range(SGU_GROUPS):
            sguwm_scr[g] = jnp.where(row >= col, sguw_ref[g], 0.0).astype(bf16)

    @pl.when(lax.rem(jnp.minimum(step, n_tiles - 1), tiles_per_seq) == 0)
    def _():
        aext_scr[:, 0:HALO, :] = jnp.zeros((D // LANES, HALO, LANES), jnp.float32)

    def ffn_up(cb):
        c0 = cb * COLS
        f = jnp.maximum(_dot(h2_scr[...], w1_scr[:, c0:c0 + COLS]), 0.0)
        f_scr[:, c0:c0 + COLS] = (f * f).astype(bf16)

    def ffn_down(cb):
        c0 = cb * COLS
        o_ref[:, c0:c0 + COLS] = x1_scr[:, c0:c0 + COLS] + _dot(f_scr[...], w2_scr[:, c0:c0 + COLS])

    def ffn_finish():
        if final_norm:
            o_ref[...] = _rmsnorm(o_ref[...], nfg_ref[...])

    def mixer_start():
        h_scr[...] = _rmsnorm(x_ref[...], n1g_ref[...]).astype(bf16)

    def in_proj(col0):
        return _dot(h_scr[...], win_scr[:, col0:col0 + COLS]) + bin_ref[:, col0:col0 + COLS]

    def glu_block(cb):
        c0 = cb * COLS
        a = in_proj(2 * D + c0) * _sigmoid(in_proj(3 * D + c0))
        for j in range(COLS // LANES):
            aext_scr[c0 // LANES + j, HALO:HALO + T, :] = a[:, j * LANES:(j + 1) * LANES]

    def conv_block(lb):
        tap0 = HALO - (CONV_KERNEL - 1)
        l0 = lb * LANES
        for rb in range(T // CONV_ROWS):
            r0 = rb * CONV_ROWS
            acc = jnp.zeros((CONV_ROWS, LANES), jnp.float32) + convb_ref[:, l0:l0 + LANES]
            for k in range(CONV_KERNEL):
                a_k = aext_scr[lb, pl.ds(r0 + tap0 + k, CONV_ROWS, stride=1), :]
                acc = acc + convw_ref[k:k + 1, l0:l0 + LANES] * a_k
            c_scr[r0:r0 + CONV_ROWS, l0:l0 + LANES] = acc

    def gelu_block(dst_scr, seg, cb):
        c0 = cb * COLS
        dst_scr[:, c0:c0 + COLS] = _gelu_tanh(in_proj(seg * D + c0))

    def gate_block(dst_scr, seg, cb):
        c0 = cb * COLS
        dst_scr[:, c0:c0 + COLS] = _sigmoid(in_proj(seg * D + c0))

    def halo_keep():
        aext_scr[:, 0:HALO, :] = aext_scr[:, T:T + HALO, :]

    def spatial_gating():
        vn_scr[...] = _layernorm(v_scr[...], lnag_ref[...], lnab_ref[...]).astype(bf16)
        n_chunks = T // CHUNK
        for g in range(SGU_GROUPS):
            c0 = g * LANES
            vg = jnp.concatenate(
                [vn_scr[n * CHUNK:(n + 1) * CHUNK, c0:c0 + LANES] for n in range(n_chunks)], axis=1)
            mixed = _dot(sguwm_scr[g], vg)
            for n in range(n_chunks):
                r0 = n * CHUNK
                sg_scr[r0:r0 + CHUNK, c0:c0 + LANES] = (
                    u_scr[r0:r0 + CHUNK, c0:c0 + LANES]
                    * (mixed[:, n * LANES:(n + 1) * LANES] + sgub_ref[g])).astype(bf16)

    def conv_tail():
        cn = _layernorm(c_scr[...], lncg_ref[...], lncb_ref[...])
        cn_scr[...] = (cn * _sigmoid(cn)).astype(bf16)

    def merge():
        y_a = _dot(sg_scr[...], wpa_scr[...])
        y_b = _dot(cn_scr[...], wpb_scr[...]) + bpb_ref[...]
        m_scr[...] = (g0_scr[...] * y_a + g1_scr[...] * y_b).astype(bf16)

    def mixer_finish():
        x1 = x_ref[...] + _dot(m_scr[...], wout_scr[...])
        x1_scr[...] = x1
        h2_scr[...] = _rmsnorm(x1, n2g_ref[...]).astype(bf16)

    n_in = D // COLS
    ffn_work = ([functools.partial(ffn_up, cb) for cb in range(f_scr.shape[1] // COLS)]
                + [functools.partial(ffn_down, cb) for cb in range(n_in)])

    def next_ffn():
        if ffn_work:
            ffn_work.pop(0)()

    next_ffn()
    mixer_start()
    for cb in range(n_in):
        next_ffn()
        glu_block(cb)
    for lb in range(D // LANES):
        conv_block(lb)
        next_ffn()
    while ffn_work:
        next_ffn()
    halo_keep()
    ffn_finish()
    for cb in range(n_in):
        gelu_block(u_scr, 0, cb)
    for cb in range(n_in):
        if cb == n_in - 1:
            conv_tail()
        gelu_block(v_scr, 1, cb)
    gate_block(g0_scr, 4, 0)
    spatial_gating()
    for cb in range(1, n_in):
        gate_block(g0_scr, 4, cb)
    for cb in range(n_in):
        gate_block(g1_scr, 5, cb)
    merge()
    mixer_finish()


def _resident(shape):
    zeros = (0,) * len(shape)
    return pl.BlockSpec(shape, lambda *_: zeros, pipeline_mode=pl.Buffered(1))


def _nbytes(shape, dtype):
    return math.prod(shape) * jnp.dtype(dtype).itemsize


def _vmem_limit(resident, streamed, scratch, temporaries):
    total = sum(_nbytes(*b) for b in resident) + 2 * sum(_nbytes(*b) for b in streamed)
    total += sum(_nbytes(*b) for b in scratch) + sum(_nbytes(*b) for b in temporaries)
    assert total <= VMEM_CAPACITY, total
    return total


def _block(x2d, layer, tiles_per_seq, final_norm, params, weights):
    M, D = x2d.shape
    F = weights[4].shape[2]
    T = TOKENS
    assert M % T == 0 and T % CHUNK == 0 and D % COLS == 0 and F % COLS == 0 and T % CONV_ROWS == 0
    n_tiles = M // T
    f32, bf16 = jnp.float32, jnp.bfloat16
    weight_buffers = [(w.shape[1:], bf16) for w in weights] + [((STAGE_SLOTS, STAGE_ROWS, STAGE_COLS), f32)]
    scratch = [
        ((T, D), bf16),
        ((T, D), f32),
        ((T, D), f32),
        ((T, D), bf16),
        ((T, D), bf16),
        ((D // LANES, HALO + T, LANES), f32),
        ((T, D), f32),
        ((T, D), bf16),
        ((T, D), f32),
        ((T, D), f32),
        ((T, D), bf16),
        ((T, D), f32),
        ((T, D), bf16),
        ((T, F), bf16),
        ((SGU_GROUPS, CHUNK, CHUNK), bf16),
    ]
    limit = _vmem_limit(
        resident=[(p.shape, p.dtype) for p in params],
        streamed=[((T, D), f32), ((T, D), f32)],
        scratch=weight_buffers + scratch,
        temporaries=[((T, D), f32)] * 2)
    return pl.pallas_call(
        functools.partial(_block_kernel, layer=layer, final_norm=final_norm, n_tiles=n_tiles,
                          tiles_per_seq=tiles_per_seq),
        grid=(n_tiles + 1,),
        in_specs=[pl.BlockSpec((T, D), lambda i: (jnp.minimum(i, n_tiles - 1), 0))]
                 + [_resident(p.shape) for p in params]
                 + [pl.BlockSpec(memory_space=pl.ANY)] * len(weights),
        out_specs=pl.BlockSpec((T, D), lambda i: (jnp.maximum(i - 1, 0), 0)),
        out_shape=jax.ShapeDtypeStruct(x2d.shape, x2d.dtype),
        scratch_shapes=[pltpu.VMEM(shape, dtype) for shape, dtype in weight_buffers]
                       + [pltpu.SemaphoreType.DMA((STAGE_SLOTS,))]
                       + [pltpu.VMEM(shape, dtype) for shape, dtype in scratch],
        compiler_params=pltpu.CompilerParams(
            dimension_semantics=("arbitrary",), vmem_limit_bytes=limit),
        name="block",
    )(x2d, *params, *weights)


def kernel(x, norm1_g, w_in, b_in, sgu_ln_g, sgu_ln_b, sgu_w, sgu_b, w_proj_a, conv_w, conv_b,
           conv_ln_g, conv_ln_b, w_proj_b, b_proj_b, w_out, norm2_g, w_ff1, w_ff2, norm_f_g):
    B, S, D = x.shape
    depth = w_in.shape[0]
    row = lambda p: p.reshape(1, -1)
    x2d = x.reshape(B * S, D)
    for l in range(depth):
        sgub = jnp.broadcast_to(sgu_b[l][:, :, None], (SGU_GROUPS, CHUNK, LANES))
        params = (row(norm1_g[l]), row(b_in[l]), row(sgu_ln_g[l]), row(sgu_ln_b[l]), sgu_w[l], sgub,
                  conv_w[l], row(conv_b[l]), row(conv_ln_g[l]), row(conv_ln_b[l]), row(b_proj_b[l]),
                  row(norm2_g[l]), row(norm_f_g))
        x2d = _block(x2d, l, S // TOKENS, l == depth - 1, params,
                     (w_in, w_proj_a, w_proj_b, w_out, w_ff1, w_ff2))
    return x2d.reshape(B, S, D)
```

```python
import functools
import math

import jax
import jax.numpy as jnp
from jax import lax
from jax.experimental import pallas as pl
from jax.experimental.pallas import tpu as pltpu

CHUNK = 128
SGU_GROUPS = 8
CONV_KERNEL = 31
EPS = 1e-6

LANES = 128
HALO = 32
TOKENS = 256
COLS = 512
CONV_ROWS = 16
STAGE_ROWS, STAGE_COLS = 128, 1024
STAGE_SLOTS = 4
VMEM_CAPACITY = 64 * 1024 * 1024

_GELU_C = math.sqrt(2.0 / math.pi)


def _dot(a, b):
    return jnp.dot(a, b, preferred_element_type=jnp.float32)


def _sigmoid(x):
    return _sigmoid_of_twice(0.5 * x)


def _sigmoid_of_twice(half_x):
    return 0.5 * jnp.tanh(half_x) + 0.5


def _gelu_tanh(x):
    half = 0.5 * x
    return half + half * jnp.tanh(x * (_GELU_C + (_GELU_C * 0.044715) * (x * x)))


def _rmsnorm(x, g):
    return x * lax.rsqrt(jnp.mean(x * x, axis=-1, keepdims=True) + EPS) * g


def _layernorm(x, g, b):
    mu = jnp.mean(x, axis=-1, keepdims=True)
    xc = x - mu
    var = jnp.mean(xc * xc, axis=-1, keepdims=True)
    return xc * lax.rsqrt(var + EPS) * g + b


def _stage_weights(layer, pairs, stage_scr, sem):
    chunks = []
    for src, dst, halve_from_col in pairs:
        rows, cols = dst.shape
        assert rows % STAGE_ROWS == 0 and cols % STAGE_COLS == 0
        assert halve_from_col is None or halve_from_col % STAGE_COLS == 0
        for r0 in range(0, rows, STAGE_ROWS):
            for c0 in range(0, cols, STAGE_COLS):
                halve = halve_from_col is not None and c0 >= halve_from_col
                chunks.append((src, dst, r0, c0, halve))

    def copy(j):
        src, _, r0, c0, _ = chunks[j]
        return pltpu.make_async_copy(
            src.at[layer, pl.ds(r0, STAGE_ROWS), pl.ds(c0, STAGE_COLS)],
            stage_scr.at[j % STAGE_SLOTS], sem.at[j % STAGE_SLOTS])

    lookahead = STAGE_SLOTS - 1
    for j in range(min(lookahead, len(chunks))):
        copy(j).start()
    for j, (_, dst, r0, c0, halve) in enumerate(chunks):
        if j + lookahead < len(chunks):
            copy(j + lookahead).start()
        copy(j).wait()
        w = stage_scr[j % STAGE_SLOTS]
        dst[r0:r0 + STAGE_ROWS, c0:c0 + STAGE_COLS] = (0.5 * w if halve else w).astype(jnp.bfloat16)


def _block_kernel(x_ref, n1g_ref, bin_ref, lnag_ref, lnab_ref, sguw_ref, sgub_ref,
                  convw_ref, convb_ref, lncg_ref, lncb_ref, bpb_ref, n2g_ref, nfg_ref,
                  win_hbm, wpa_hbm, wpb_hbm, wout_hbm, w1_hbm, w2_hbm,
                  o_ref,
                  win_scr, wpa_scr, wpb_scr, wout_scr, w1_scr, w2_scr, stage_scr, stage_sem,
                  h_scr, u_scr, v_scr, vn_scr, sg_scr, aext_scr, c_scr, cn_scr, g0_scr, g1_scr, m_scr,
                  x1_scr, h2_scr, f_scr, sguwm_scr,
                  *, layer, final_norm, n_tiles, tiles_per_seq):
    T, D = h_scr.shape
    bf16 = jnp.bfloat16
    step = pl.program_id(0)

    @pl.when(step == 0)
    def _():
        _stage_weights(layer, [(win_hbm, win_scr, 3 * D), (w1_hbm, w1_scr, None),
                               (wpa_hbm, wpa_scr, None), (wpb_hbm, wpb_scr, None),
                               (wout_hbm, wout_scr, None), (w2_hbm, w2_scr, None)],
                       stage_scr, stage_sem)
        x1_scr[...] = jnp.zeros(x1_scr.shape, x1_scr.dtype)
        h2_scr[...] = jnp.zeros(h2_scr.shape, h2_scr.dtype)
        row = lax.broadcasted_iota(jnp.int32, (CHUNK, CHUNK), 0)
        col = lax.broadcasted_iota(jnp.int32, (CHUNK, CHUNK), 1)
        for g in range(SGU_GROUPS):
            sguwm_scr[g] = jnp.where(row >= col, sguw_ref[g], 0.0).astype(bf16)

    @pl.when(lax.rem(jnp.minimum(step, n_tiles - 1), tiles_per_seq) == 0)
    def _():
        aext_scr[:, 0:HALO, :] = jnp.zeros((D // LANES, HALO, LANES), jnp.float32)

    def ffn_up(cb):
        c0 = cb * COLS
        f = jnp.maximum(_dot(h2_scr[...], w1_scr[:, c0:c0 + COLS]), 0.0)
        f_scr[:, c0:c0 + COLS] = (f * f).astype(bf16)

    def ffn_down(cb):
        c0 = cb * COLS
        o_ref[:, c0:c0 + COLS] = x1_scr[:, c0:c0 + COLS] + _dot(f_scr[...], w2_scr[:, c0:c0 + COLS])

    def ffn_finish():
        if final_norm:
            o_ref[...] = _rmsnorm(o_ref[...], nfg_ref[...])

    def mixer_start():
        h_scr[...] = _rmsnorm(x_ref[...], n1g_ref[...]).astype(bf16)

    def in_proj(col0):
        return _dot(h_scr[...], win_scr[:, col0:col0 + COLS]) + bin_ref[:, col0:col0 + COLS]

    def half_in_proj(col0):
        assert col0 >= 3 * D
        return _dot(h_scr[...], win_scr[:, col0:col0 + COLS]) + 0.5 * bin_ref[:, col0:col0 + COLS]

    def glu_block(cb):
        c0 = cb * COLS
        a = in_proj(2 * D + c0) * _sigmoid_of_twice(half_in_proj(3 * D + c0))
        for j in range(COLS // LANES):
            aext_scr[c0 // LANES + j, HALO:HALO + T, :] = a[:, j * LANES:(j + 1) * LANES]

    def conv_block(lb):
        tap0 = HALO - (CONV_KERNEL - 1)
        l0 = lb * LANES
        for rb in range(T // CONV_ROWS):
            r0 = rb * CONV_ROWS
            acc = jnp.zeros((CONV_ROWS, LANES), jnp.float32) + convb_ref[:, l0:l0 + LANES]
            for k in range(CONV_KERNEL):
                a_k = aext_scr[lb, pl.ds(r0 + tap0 + k, CONV_ROWS, stride=1), :]
                acc = acc + convw_ref[k:k + 1, l0:l0 + LANES] * a_k
            c_scr[r0:r0 + CONV_ROWS, l0:l0 + LANES] = acc

    def gelu_block(dst_scr, seg, cb):
        c0 = cb * COLS
        dst_scr[:, c0:c0 + COLS] = _gelu_tanh(in_proj(seg * D + c0))

    def gate_block(dst_scr, seg, cb):
        c0 = cb * COLS
        dst_scr[:, c0:c0 + COLS] = _sigmoid_of_twice(half_in_proj(seg * D + c0))

    def halo_keep():
        aext_scr[:, 0:HALO, :] = aext_scr[:, T:T + HALO, :]

    def spatial_gating():
        vn_scr[...] = _layernorm(v_scr[...], lnag_ref[...], lnab_ref[...]).astype(bf16)
        n_chunks = T // CHUNK
        for g in range(SGU_GROUPS):
            c0 = g * LANES
            vg = jnp.concatenate(
                [vn_scr[n * CHUNK:(n + 1) * CHUNK, c0:c0 + LANES] for n in range(n_chunks)], axis=1)
            mixed = _dot(sguwm_scr[g], vg)
            for n in range(n_chunks):
                r0 = n * CHUNK
                sg_scr[r0:r0 + CHUNK, c0:c0 + LANES] = (
                    u_scr[r0:r0 + CHUNK, c0:c0 + LANES]
                    * (mixed[:, n * LANES:(n + 1) * LANES] + sgub_ref[g])).astype(bf16)

    def conv_tail():
        cn = _layernorm(c_scr[...], lncg_ref[...], lncb_ref[...])
        cn_scr[...] = (cn * _sigmoid(cn)).astype(bf16)

    def merge():
        y_a = _dot(sg_scr[...], wpa_scr[...])
        y_b = _dot(cn_scr[...], wpb_scr[...]) + bpb_ref[...]
        m_scr[...] = (g0_scr[...] * y_a + g1_scr[...] * y_b).astype(bf16)

    def mixer_finish():
        x1 = x_ref[...] + _dot(m_scr[...], wout_scr[...])
        x1_scr[...] = x1
        h2_scr[...] = _rmsnorm(x1, n2g_ref[...]).astype(bf16)

    n_in = D // COLS
    ffn_work = ([functools.partial(ffn_up, cb) for cb in range(f_scr.shape[1] // COLS)]
                + [functools.partial(ffn_down, cb) for cb in range(n_in)])

    def next_ffn():
        if ffn_work:
            ffn_work.pop(0)()

    next_ffn()
    mixer_start()
    for cb in range(n_in):
        next_ffn()
        glu_block(cb)
    for lb in range(D // LANES):
        conv_block(lb)
        next_ffn()
    while ffn_work:
        next_ffn()
    halo_keep()
    ffn_finish()
    for cb in range(n_in):
        gelu_block(u_scr, 0, cb)
    for cb in range(n_in):
        if cb == n_in - 1:
            conv_tail()
        gelu_block(v_scr, 1, cb)
    gate_block(g0_scr, 4, 0)
    spatial_gating()
    for cb in range(1, n_in):
        gate_block(g0_scr, 4, cb)
    for cb in range(n_in):
        gate_block(g1_scr, 5, cb)
    merge()
    mixer_finish()


def _resident(shape):
    zeros = (0,) * len(shape)
    return pl.BlockSpec(shape, lambda *_: zeros, pipeline_mode=pl.Buffered(1))


def _nbytes(shape, dtype):
    return math.prod(shape) * jnp.dtype(dtype).itemsize


def _vmem_limit(resident, streamed, scratch, temporaries):
    total = sum(_nbytes(*b) for b in resident) + 2 * sum(_nbytes(*b) for b in streamed)
    total += sum(_nbytes(*b) for b in scratch) + sum(_nbytes(*b) for b in temporaries)
    assert total <= VMEM_CAPACITY, total
    return total


def _block(x2d, layer, tiles_per_seq, final_norm, params, weights):
    M, D = x2d.shape
    F = weights[4].shape[2]
    T = TOKENS
    assert M % T == 0 and T % CHUNK == 0 and D % COLS == 0 and F % COLS == 0 and T % CONV_ROWS == 0
    n_tiles = M // T
    f32, bf16 = jnp.float32, jnp.bfloat16
    weight_buffers = [(w.shape[1:], bf16) for w in weights] + [((STAGE_SLOTS, STAGE_ROWS, STAGE_COLS), f32)]
    scratch = [
        ((T, D), bf16),
        ((T, D), f32),
        ((T, D), f32),
        ((T, D), bf16),
        ((T, D), bf16),
        ((D // LANES, HALO + T, LANES), f32),
        ((T, D), f32),
        ((T, D), bf16),
        ((T, D), f32),
        ((T, D), f32),
        ((T, D), bf16),
        ((T, D), f32),
        ((T, D), bf16),
        ((T, F), bf16),
        ((SGU_GROUPS, CHUNK, CHUNK), bf16),
    ]
    limit = _vmem_limit(
        resident=[(p.shape, p.dtype) for p in params],
        streamed=[((T, D), f32), ((T, D), f32)],
        scratch=weight_buffers + scratch,
        temporaries=[((T, D), f32)] * 2)
    return pl.pallas_call(
        functools.partial(_block_kernel, layer=layer, final_norm=final_norm, n_tiles=n_tiles,
                          tiles_per_seq=tiles_per_seq),
        grid=(n_tiles + 1,),
        in_specs=[pl.BlockSpec((T, D), lambda i: (jnp.minimum(i, n_tiles - 1), 0))]
                 + [_resident(p.shape) for p in params]
                 + [pl.BlockSpec(memory_space=pl.ANY)] * len(weights),
        out_specs=pl.BlockSpec((T, D), lambda i: (jnp.maximum(i - 1, 0), 0)),
        out_shape=jax.ShapeDtypeStruct(x2d.shape, x2d.dtype),
        scratch_shapes=[pltpu.VMEM(shape, dtype) for shape, dtype in weight_buffers]
                       + [pltpu.SemaphoreType.DMA((STAGE_SLOTS,))]
                       + [pltpu.VMEM(shape, dtype) for shape, dtype in scratch],
        compiler_params=pltpu.CompilerParams(
            dimension_semantics=("arbitrary",), vmem_limit_bytes=limit),
        name="block",
    )(x2d, *params, *weights)


def kernel(x, norm1_g, w_in, b_in, sgu_ln_g, sgu_ln_b, sgu_w, sgu_b, w_proj_a, conv_w, conv_b,
           conv_ln_g, conv_ln_b, w_proj_b, b_proj_b, w_out, norm2_g, w_ff1, w_ff2, norm_f_g):
    B, S, D = x.shape
    depth = w_in.shape[0]
    row = lambda p: p.reshape(1, -1)
    x2d = x.reshape(B * S, D)
    for l in range(depth):
        sgub = jnp.broadcast_to(sgu_b[l][:, :, None], (SGU_GROUPS, CHUNK, LANES))
        params = (row(norm1_g[l]), row(b_in[l]), row(sgu_ln_g[l]), row(sgu_ln_b[l]), sgu_w[l], sgub,
                  conv_w[l], row(conv_b[l]), row(conv_ln_g[l]), row(conv_ln_b[l]), row(b_proj_b[l]),
                  row(norm2_g[l]), row(norm_f_g))
        x2d = _block(x2d, l, S // TOKENS, l == depth - 1, params,
                     (w_in, w_proj_a, w_proj_b, w_out, w_ff1, w_ff2))
    return x2d.reshape(B, S, D)
```

```python
import functools
import math

import jax
import jax.numpy as jnp
from jax import lax
from jax.experimental import pallas as pl
from jax.experimental.pallas import tpu as pltpu

CHUNK = 128
SGU_GROUPS = 8
CONV_KERNEL = 31
EPS = 1e-6

LANES = 128
HALO = 32
TOKENS = 256
COLS = 512
CONV_ROWS = 16
STAGE_ROWS, STAGE_COLS = 128, 1024
STAGE_SLOTS = 4
VMEM_CAPACITY = 64 * 1024 * 1024

_GELU_C = math.sqrt(2.0 / math.pi)


def _dot(a, b):
    return jnp.dot(a, b, preferred_element_type=jnp.float32)


def _gelu_tanh(x):
    half = 0.5 * x
    return half + half * jnp.tanh(x * (_GELU_C + (_GELU_C * 0.044715) * (x * x)))


def _rmsnorm(x, g):
    return x * lax.rsqrt(jnp.mean(x * x, axis=-1, keepdims=True) + EPS) * g


def _layernorm(x, g, b):
    mu = jnp.mean(x, axis=-1, keepdims=True)
    xc = x - mu
    var = jnp.mean(xc * xc, axis=-1, keepdims=True)
    return xc * lax.rsqrt(var + EPS) * g + b


def _stage_weights(layer, pairs, stage_scr, sem):
    chunks = []
    for src, dst, halve_from_col in pairs:
        rows, cols = dst.shape
        assert rows % STAGE_ROWS == 0 and cols % STAGE_COLS == 0
        assert halve_from_col is None or halve_from_col % STAGE_COLS == 0
        for r0 in range(0, rows, STAGE_ROWS):
            for c0 in range(0, cols, STAGE_COLS):
                halve = halve_from_col is not None and c0 >= halve_from_col
                chunks.append((src, dst, r0, c0, halve))

    def copy(j):
        src, _, r0, c0, _ = chunks[j]
        return pltpu.make_async_copy(
            src.at[layer, pl.ds(r0, STAGE_ROWS), pl.ds(c0, STAGE_COLS)],
            stage_scr.at[j % STAGE_SLOTS], sem.at[j % STAGE_SLOTS])

    lookahead = STAGE_SLOTS - 1
    for j in range(min(lookahead, len(chunks))):
        copy(j).start()
    for j, (_, dst, r0, c0, halve) in enumerate(chunks):
        if j + lookahead < len(chunks):
            copy(j + lookahead).start()
        copy(j).wait()
        w = stage_scr[j % STAGE_SLOTS]
        dst[r0:r0 + STAGE_ROWS, c0:c0 + STAGE_COLS] = (0.5 * w if halve else w).astype(jnp.bfloat16)


def _block_kernel(x_ref, n1g_ref, bin_ref, lnag_ref, lnab_ref, sguw_ref, sgub_ref,
                  convw_ref, convb_ref, lncg_ref, lncb_ref, bpb_ref, n2g_ref, nfg_ref,
                  win_hbm, wpa_hbm, wpb_hbm, wout_hbm, w1_hbm, w2_hbm,
                  o_ref,
                  win_scr, wpa_scr, wpb_scr, wout_scr, w1_scr, w2_scr, stage_scr, stage_sem,
                  h_scr, u_scr, v_scr, vn_scr, sg_scr, aext_scr, c_scr, cn_scr, g0_scr, g1_scr, m_scr,
                  x1_scr, h2_scr, f_scr, sguwm_scr,
                  *, layer, final_norm, n_tiles, tiles_per_seq):
    T, D = h_scr.shape
    bf16 = jnp.bfloat16
    step = pl.program_id(0)

    @pl.when(step == 0)
    def _():
        _stage_weights(layer, [(win_hbm, win_scr, 2 * D), (w1_hbm, w1_scr, None),
                               (wpa_hbm, wpa_scr, None), (wpb_hbm, wpb_scr, None),
                               (wout_hbm, wout_scr, 0), (w2_hbm, w2_scr, None)],
                       stage_scr, stage_sem)
        x1_scr[...] = jnp.zeros(x1_scr.shape, x1_scr.dtype)
        h2_scr[...] = jnp.zeros(h2_scr.shape, h2_scr.dtype)
        row = lax.broadcasted_iota(jnp.int32, (CHUNK, CHUNK), 0)
        col = lax.broadcasted_iota(jnp.int32, (CHUNK, CHUNK), 1)
        for g in range(SGU_GROUPS):
            sguwm_scr[g] = jnp.where(row >= col, sguw_ref[g], 0.0).astype(bf16)

    @pl.when(lax.rem(jnp.minimum(step, n_tiles - 1), tiles_per_seq) == 0)
    def _():
        aext_scr[:, 0:HALO, :] = jnp.zeros((D // LANES, HALO, LANES), jnp.float32)

    def ffn_up(cb):
        c0 = cb * COLS
        f = jnp.maximum(_dot(h2_scr[...], w1_scr[:, c0:c0 + COLS]), 0.0)
        f_scr[:, c0:c0 + COLS] = (f * f).astype(bf16)

    def ffn_down(cb):
        c0 = cb * COLS
        o_ref[:, c0:c0 + COLS] = x1_scr[:, c0:c0 + COLS] + _dot(f_scr[...], w2_scr[:, c0:c0 + COLS])

    def ffn_finish():
        if final_norm:
            o_ref[...] = _rmsnorm(o_ref[...], nfg_ref[...])

    def mixer_start():
        h_scr[...] = _rmsnorm(x_ref[...], n1g_ref[...]).astype(bf16)

    def in_proj(col0):
        return _dot(h_scr[...], win_scr[:, col0:col0 + COLS]) + bin_ref[:, col0:col0 + COLS]

    def half_in_proj(col0):
        assert col0 >= 2 * D
        return _dot(h_scr[...], win_scr[:, col0:col0 + COLS]) + 0.5 * bin_ref[:, col0:col0 + COLS]

    def glu_block(cb):
        c0 = cb * COLS
        half_val = half_in_proj(2 * D + c0)
        a = half_val + half_val * jnp.tanh(half_in_proj(3 * D + c0))
        for j in range(COLS // LANES):
            aext_scr[c0 // LANES + j, HALO:HALO + T, :] = a[:, j * LANES:(j + 1) * LANES]

    def conv_block(lb):
        tap0 = HALO - (CONV_KERNEL - 1)
        l0 = lb * LANES
        for rb in range(T // CONV_ROWS):
            r0 = rb * CONV_ROWS
            acc = jnp.zeros((CONV_ROWS, LANES), jnp.float32) + convb_ref[:, l0:l0 + LANES]
            for k in range(CONV_KERNEL):
                a_k = aext_scr[lb, pl.ds(r0 + tap0 + k, CONV_ROWS, stride=1), :]
                acc = acc + convw_ref[k:k + 1, l0:l0 + LANES] * a_k
            c_scr[r0:r0 + CONV_ROWS, l0:l0 + LANES] = acc

    def gelu_block(dst_scr, seg, cb):
        c0 = cb * COLS
        dst_scr[:, c0:c0 + COLS] = _gelu_tanh(in_proj(seg * D + c0))

    def gate_block(dst_scr, seg, cb):
        c0 = cb * COLS
        dst_scr[:, c0:c0 + COLS] = jnp.tanh(half_in_proj(seg * D + c0))

    def halo_keep():
        aext_scr[:, 0:HALO, :] = aext_scr[:, T:T + HALO, :]

    def spatial_gating():
        vn_scr[...] = _layernorm(v_scr[...], lnag_ref[...], lnab_ref[...]).astype(bf16)
        n_chunks = T // CHUNK
        for g in range(SGU_GROUPS):
            c0 = g * LANES
            vg = jnp.concatenate(
                [vn_scr[n * CHUNK:(n + 1) * CHUNK, c0:c0 + LANES] for n in range(n_chunks)], axis=1)
            mixed = _dot(sguwm_scr[g], vg)
            for n in range(n_chunks):
                r0 = n * CHUNK
                sg_scr[r0:r0 + CHUNK, c0:c0 + LANES] = (
                    u_scr[r0:r0 + CHUNK, c0:c0 + LANES]
                    * (mixed[:, n * LANES:(n + 1) * LANES] + sgub_ref[g])).astype(bf16)

    def conv_tail():
        half_cn = 0.5 * _layernorm(c_scr[...], lncg_ref[...], lncb_ref[...])
        cn_scr[...] = (half_cn + half_cn * jnp.tanh(half_cn)).astype(bf16)

    def merge():
        y_a = _dot(sg_scr[...], wpa_scr[...])
        y_b = _dot(cn_scr[...], wpb_scr[...]) + bpb_ref[...]
        m_scr[...] = ((y_a + y_b) + (g0_scr[...] * y_a + g1_scr[...] * y_b)).astype(bf16)

    def mixer_finish():
        x1 = x_ref[...] + _dot(m_scr[...], wout_scr[...])
        x1_scr[...] = x1
        h2_scr[...] = _rmsnorm(x1, n2g_ref[...]).astype(bf16)

    n_in = D // COLS
    ffn_work = ([functools.partial(ffn_up, cb) for cb in range(f_scr.shape[1] // COLS)]
                + [functools.partial(ffn_down, cb) for cb in range(n_in)])

    def next_ffn():
        if ffn_work:
            ffn_work.pop(0)()

    next_ffn()
    mixer_start()
    for cb in range(n_in):
        next_ffn()
        glu_block(cb)
    for lb in range(D // LANES):
        conv_block(lb)
        next_ffn()
    while ffn_work:
        next_ffn()
    halo_keep()
    ffn_finish()
    for cb in range(n_in):
        gelu_block(u_scr, 0, cb)
    for cb in range(n_in):
        if cb == n_in - 1:
            conv_tail()
        gelu_block(v_scr, 1, cb)
    gate_block(g0_scr, 4, 0)
    spatial_gating()
    for cb in range(1, n_in):
        gate_block(g0_scr, 4, cb)
    for cb in range(n_in):
        gate_block(g1_scr, 5, cb)
    merge()
    mixer_finish()


def _resident(shape):
    zeros = (0,) * len(shape)
    return pl.BlockSpec(shape, lambda *_: zeros, pipeline_mode=pl.Buffered(1))


def _nbytes(shape, dtype):
    return math.prod(shape) * jnp.dtype(dtype).itemsize


def _vmem_limit(resident, streamed, scratch, temporaries):
    total = sum(_nbytes(*b) for b in resident) + 2 * sum(_nbytes(*b) for b in streamed)
    total += sum(_nbytes(*b) for b in scratch) + sum(_nbytes(*b) for b in temporaries)
    assert total <= VMEM_CAPACITY, total
    return total


def _block(x2d, layer, tiles_per_seq, final_norm, params, weights):
    M, D = x2d.shape
    F = weights[4].shape[2]
    T = TOKENS
    assert M % T == 0 and T % CHUNK == 0 and D % COLS == 0 and F % COLS == 0 and T % CONV_ROWS == 0
    n_tiles = M // T
    f32, bf16 = jnp.float32, jnp.bfloat16
    weight_buffers = [(w.shape[1:], bf16) for w in weights] + [((STAGE_SLOTS, STAGE_ROWS, STAGE_COLS), f32)]
    scratch = [
        ((T, D), bf16),
        ((T, D), f32),
        ((T, D), f32),
        ((T, D), bf16),
        ((T, D), bf16),
        ((D // LANES, HALO + T, LANES), f32),
        ((T, D), f32),
        ((T, D), bf16),
        ((T, D), f32),
        ((T, D), f32),
        ((T, D), bf16),
        ((T, D), f32),
        ((T, D), bf16),
        ((T, F), bf16),
        ((SGU_GROUPS, CHUNK, CHUNK), bf16),
    ]
    limit = _vmem_limit(
        resident=[(p.shape, p.dtype) for p in params],
        streamed=[((T, D), f32), ((T, D), f32)],
        scratch=weight_buffers + scratch,
        temporaries=[((T, D), f32)] * 2)
    return pl.pallas_call(
        functools.partial(_block_kernel, layer=layer, final_norm=final_norm, n_tiles=n_tiles,
                          tiles_per_seq=tiles_per_seq),
        grid=(n_tiles + 1,),
        in_specs=[pl.BlockSpec((T, D), lambda i: (jnp.minimum(i, n_tiles - 1), 0))]
                 + [_resident(p.shape) for p in params]
                 + [pl.BlockSpec(memory_space=pl.ANY)] * len(weights),
        out_specs=pl.BlockSpec((T, D), lambda i: (jnp.maximum(i - 1, 0), 0)),
        out_shape=jax.ShapeDtypeStruct(x2d.shape, x2d.dtype),
        scratch_shapes=[pltpu.VMEM(shape, dtype) for shape, dtype in weight_buffers]
                       + [pltpu.SemaphoreType.DMA((STAGE_SLOTS,))]
                       + [pltpu.VMEM(shape, dtype) for shape, dtype in scratch],
        compiler_params=pltpu.CompilerParams(
            dimension_semantics=("arbitrary",), vmem_limit_bytes=limit),
        name="block",
    )(x2d, *params, *weights)


def kernel(x, norm1_g, w_in, b_in, sgu_ln_g, sgu_ln_b, sgu_w, sgu_b, w_proj_a, conv_w, conv_b,
           conv_ln_g, conv_ln_b, w_proj_b, b_proj_b, w_out, norm2_g, w_ff1, w_ff2, norm_f_g):
    B, S, D = x.shape
    depth = w_in.shape[0]
    row = lambda p: p.reshape(1, -1)
    x2d = x.reshape(B * S, D)
    for l in range(depth):
        sgub = jnp.broadcast_to(sgu_b[l][:, :, None], (SGU_GROUPS, CHUNK, LANES))
        params = (row(norm1_g[l]), row(b_in[l]), row(sgu_ln_g[l]), row(sgu_ln_b[l]), sgu_w[l], sgub,
                  conv_w[l], row(conv_b[l]), row(conv_ln_g[l]), row(conv_ln_b[l]), row(b_proj_b[l]),
                  row(norm2_g[l]), row(norm_f_g))
        x2d = _block(x2d, l, S // TOKENS, l == depth - 1, params,
                     (w_in, w_proj_a, w_proj_b, w_out, w_ff1, w_ff2))
    return x2d.reshape(B, S, D)
```

```python
import functools
import math

import jax
import jax.numpy as jnp
from jax import lax
from jax.experimental import pallas as pl
from jax.experimental.pallas import tpu as pltpu

CHUNK = 128
SGU_GROUPS = 8
CONV_KERNEL = 31
EPS = 1e-6

LANES = 128
HALO = 32
TOKENS = 256
COLS = 512
CONV_ROWS = 16
STAGE_ROWS, STAGE_COLS = 128, 1024
STAGE_SLOTS = 4
VMEM_CAPACITY = 64 * 1024 * 1024

_GELU_C = math.sqrt(2.0 / math.pi)


def _dot(a, b):
    return jnp.dot(a, b, preferred_element_type=jnp.float32)


def _gelu_tanh_of_twice(half):
    return half + half * jnp.tanh(half * (2.0 * _GELU_C + (8.0 * _GELU_C * 0.044715) * (half * half)))


def _rmsnorm(x, g):
    return x * lax.rsqrt(jnp.mean(x * x, axis=-1, keepdims=True) + EPS) * g


def _layernorm(x, g, b):
    mu = jnp.mean(x, axis=-1, keepdims=True)
    xc = x - mu
    var = jnp.mean(xc * xc, axis=-1, keepdims=True)
    return xc * lax.rsqrt(var + EPS) * g + b


def _stage_weights(layer, pairs, stage_scr, sem):
    chunks = []
    for src, dst, halve_from_col in pairs:
        rows, cols = dst.shape
        assert rows % STAGE_ROWS == 0 and cols % STAGE_COLS == 0
        assert halve_from_col is None or halve_from_col % STAGE_COLS == 0
        for r0 in range(0, rows, STAGE_ROWS):
            for c0 in range(0, cols, STAGE_COLS):
                halve = halve_from_col is not None and c0 >= halve_from_col
                chunks.append((src, dst, r0, c0, halve))

    def copy(j):
        src, _, r0, c0, _ = chunks[j]
        return pltpu.make_async_copy(
            src.at[layer, pl.ds(r0, STAGE_ROWS), pl.ds(c0, STAGE_COLS)],
            stage_scr.at[j % STAGE_SLOTS], sem.at[j % STAGE_SLOTS])

    lookahead = STAGE_SLOTS - 1
    for j in range(min(lookahead, len(chunks))):
        copy(j).start()
    for j, (_, dst, r0, c0, halve) in enumerate(chunks):
        if j + lookahead < len(chunks):
            copy(j + lookahead).start()
        copy(j).wait()
        w = stage_scr[j % STAGE_SLOTS]
        dst[r0:r0 + STAGE_ROWS, c0:c0 + STAGE_COLS] = (0.5 * w if halve else w).astype(jnp.bfloat16)


def _block_kernel(x_ref, n1g_ref, bin_ref, lnag_ref, lnab_ref, sguw_ref, sgub_ref,
                  convw_ref, convb_ref, lncg_ref, lncb_ref, bpb_ref, n2g_ref, nfg_ref,
                  win_hbm, wpa_hbm, wpb_hbm, wout_hbm, w1_hbm, w2_hbm,
                  o_ref,
                  win_scr, wpa_scr, wpb_scr, wout_scr, w1_scr, w2_scr, stage_scr, stage_sem,
                  h_scr, u_scr, v_scr, vn_scr, sg_scr, aext_scr, c_scr, cn_scr, g0_scr, g1_scr, m_scr,
                  x1_scr, h2_scr, f_scr, sguwm_scr,
                  *, layer, final_norm, n_tiles, tiles_per_seq):
    T, D = h_scr.shape
    bf16 = jnp.bfloat16
    step = pl.program_id(0)

    @pl.when(step == 0)
    def _():
        _stage_weights(layer, [(win_hbm, win_scr, 0), (w1_hbm, w1_scr, None),
                               (wpa_hbm, wpa_scr, None), (wpb_hbm, wpb_scr, None),
                               (wout_hbm, wout_scr, 0), (w2_hbm, w2_scr, None)],
                       stage_scr, stage_sem)
        x1_scr[...] = jnp.zeros(x1_scr.shape, x1_scr.dtype)
        h2_scr[...] = jnp.zeros(h2_scr.shape, h2_scr.dtype)
        row = lax.broadcasted_iota(jnp.int32, (CHUNK, CHUNK), 0)
        col = lax.broadcasted_iota(jnp.int32, (CHUNK, CHUNK), 1)
        for g in range(SGU_GROUPS):
            sguwm_scr[g] = jnp.where(row >= col, sguw_ref[g], 0.0).astype(bf16)

    @pl.when(lax.rem(jnp.minimum(step, n_tiles - 1), tiles_per_seq) == 0)
    def _():
        aext_scr[:, 0:HALO, :] = jnp.zeros((D // LANES, HALO, LANES), jnp.float32)

    def ffn_up(cb):
        c0 = cb * COLS
        f = jnp.maximum(_dot(h2_scr[...], w1_scr[:, c0:c0 + COLS]), 0.0)
        f_scr[:, c0:c0 + COLS] = (f * f).astype(bf16)

    def ffn_down(cb):
        c0 = cb * COLS
        o_ref[:, c0:c0 + COLS] = x1_scr[:, c0:c0 + COLS] + _dot(f_scr[...], w2_scr[:, c0:c0 + COLS])

    def ffn_finish():
        if final_norm:
            o_ref[...] = _rmsnorm(o_ref[...], nfg_ref[...])

    def mixer_start():
        h_scr[...] = _rmsnorm(x_ref[...], n1g_ref[...]).astype(bf16)

    def half_in_proj(col0):
        return _dot(h_scr[...], win_scr[:, col0:col0 + COLS]) + 0.5 * bin_ref[:, col0:col0 + COLS]

    def glu_block(cb):
        c0 = cb * COLS
        half_val = half_in_proj(2 * D + c0)
        a = half_val + half_val * jnp.tanh(half_in_proj(3 * D + c0))
        for j in range(COLS // LANES):
            aext_scr[c0 // LANES + j, HALO:HALO + T, :] = a[:, j * LANES:(j + 1) * LANES]

    def conv_block(lb):
        tap0 = HALO - (CONV_KERNEL - 1)
        l0 = lb * LANES
        for rb in range(T // CONV_ROWS):
            r0 = rb * CONV_ROWS
            acc = jnp.zeros((CONV_ROWS, LANES), jnp.float32) + convb_ref[:, l0:l0 + LANES]
            for k in range(CONV_KERNEL):
                a_k = aext_scr[lb, pl.ds(r0 + tap0 + k, CONV_ROWS, stride=1), :]
                acc = acc + convw_ref[k:k + 1, l0:l0 + LANES] * a_k
            c_scr[r0:r0 + CONV_ROWS, l0:l0 + LANES] = acc

    def gelu_block(dst_scr, seg, cb):
        c0 = cb * COLS
        dst_scr[:, c0:c0 + COLS] = _gelu_tanh_of_twice(half_in_proj(seg * D + c0))

    def gate_block(dst_scr, seg, cb):
        c0 = cb * COLS
        dst_scr[:, c0:c0 + COLS] = jnp.tanh(half_in_proj(seg * D + c0))

    def halo_keep():
        aext_scr[:, 0:HALO, :] = aext_scr[:, T:T + HALO, :]

    def spatial_gating():
        vn_scr[...] = _layernorm(v_scr[...], lnag_ref[...], lnab_ref[...]).astype(bf16)
        n_chunks = T // CHUNK
        for g in range(SGU_GROUPS):
            c0 = g * LANES
            vg = jnp.concatenate(
                [vn_scr[n * CHUNK:(n + 1) * CHUNK, c0:c0 + LANES] for n in range(n_chunks)], axis=1)
            mixed = _dot(sguwm_scr[g], vg)
            for n in range(n_chunks):
                r0 = n * CHUNK
                sg_scr[r0:r0 + CHUNK, c0:c0 + LANES] = (
                    u_scr[r0:r0 + CHUNK, c0:c0 + LANES]
                    * (mixed[:, n * LANES:(n + 1) * LANES] + sgub_ref[g])).astype(bf16)

    def conv_tail():
        half_cn = 0.5 * _layernorm(c_scr[...], lncg_ref[...], lncb_ref[...])
        cn_scr[...] = (half_cn + half_cn * jnp.tanh(half_cn)).astype(bf16)

    def merge():
        y_a = _dot(sg_scr[...], wpa_scr[...])
        y_b = _dot(cn_scr[...], wpb_scr[...]) + bpb_ref[...]
        m_scr[...] = ((y_a + y_b) + (g0_scr[...] * y_a + g1_scr[...] * y_b)).astype(bf16)

    def mixer_finish():
        x1 = x_ref[...] + _dot(m_scr[...], wout_scr[...])
        x1_scr[...] = x1
        h2_scr[...] = _rmsnorm(x1, n2g_ref[...]).astype(bf16)

    n_in = D // COLS
    ffn_work = ([functools.partial(ffn_up, cb) for cb in range(f_scr.shape[1] // COLS)]
                + [functools.partial(ffn_down, cb) for cb in range(n_in)])

    def next_ffn():
        if ffn_work:
            ffn_work.pop(0)()

    next_ffn()
    mixer_start()
    for cb in range(n_in):
        next_ffn()
        glu_block(cb)
    for lb in range(D // LANES):
        conv_block(lb)
        next_ffn()
    while ffn_work:
        next_ffn()
    halo_keep()
    ffn_finish()
    for cb in range(n_in):
        gelu_block(u_scr, 0, cb)
    for cb in range(n_in):
        if cb == n_in - 1:
            conv_tail()
        gelu_block(v_scr, 1, cb)
    gate_block(g0_scr, 4, 0)
    spatial_gating()
    for cb in range(1, n_in):
        gate_block(g0_scr, 4, cb)
    for cb in range(n_in):
        gate_block(g1_scr, 5, cb)
    merge()
    mixer_finish()


def _resident(shape):
    zeros = (0,) * len(shape)
    return pl.BlockSpec(shape, lambda *_: zeros, pipeline_mode=pl.Buffered(1))


def _nbytes(shape, dtype):
    return math.prod(shape) * jnp.dtype(dtype).itemsize


def _vmem_limit(resident, streamed, scratch, temporaries):
    total = sum(_nbytes(*b) for b in resident) + 2 * sum(_nbytes(*b) for b in streamed)
    total += sum(_nbytes(*b) for b in scratch) + sum(_nbytes(*b) for b in temporaries)
    assert total <= VMEM_CAPACITY, total
    return total


def _block(x2d, layer, tiles_per_seq, final_norm, params, weights):
    M, D = x2d.shape
    F = weights[4].shape[2]
    T = TOKENS
    assert M % T == 0 and T % CHUNK == 0 and D % COLS == 0 and F % COLS == 0 and T % CONV_ROWS == 0
    n_tiles = M // T
    f32, bf16 = jnp.float32, jnp.bfloat16
    weight_buffers = [(w.shape[1:], bf16) for w in weights] + [((STAGE_SLOTS, STAGE_ROWS, STAGE_COLS), f32)]
    scratch = [
        ((T, D), bf16),
        ((T, D), f32),
        ((T, D), f32),
        ((T, D), bf16),
        ((T, D), bf16),
        ((D // LANES, HALO + T, LANES), f32),
        ((T, D), f32),
        ((T, D), bf16),
        ((T, D), f32),
        ((T, D), f32),
        ((T, D), bf16),
        ((T, D), f32),
        ((T, D), bf16),
        ((T, F), bf16),
        ((SGU_GROUPS, CHUNK, CHUNK), bf16),
    ]
    limit = _vmem_limit(
        resident=[(p.shape, p.dtype) for p in params],
        streamed=[((T, D), f32), ((T, D), f32)],
        scratch=weight_buffers + scratch,
        temporaries=[((T, D), f32)] * 2)
    return pl.pallas_call(
        functools.partial(_block_kernel, layer=layer, final_norm=final_norm, n_tiles=n_tiles,
                          tiles_per_seq=tiles_per_seq),
        grid=(n_tiles + 1,),
        in_specs=[pl.BlockSpec((T, D), lambda i: (jnp.minimum(i, n_tiles - 1), 0))]
                 + [_resident(p.shape) for p in params]
                 + [pl.BlockSpec(memory_space=pl.ANY)] * len(weights),
        out_specs=pl.BlockSpec((T, D), lambda i: (jnp.maximum(i - 1, 0), 0)),
        out_shape=jax.ShapeDtypeStruct(x2d.shape, x2d.dtype),
        scratch_shapes=[pltpu.VMEM(shape, dtype) for shape, dtype in weight_buffers]
                       + [pltpu.SemaphoreType.DMA((STAGE_SLOTS,))]
                       + [pltpu.VMEM(shape, dtype) for shape, dtype in scratch],
        compiler_params=pltpu.CompilerParams(
            dimension_semantics=("arbitrary",), vmem_limit_bytes=limit),
        name="block",
    )(x2d, *params, *weights)


def kernel(x, norm1_g, w_in, b_in, sgu_ln_g, sgu_ln_b, sgu_w, sgu_b, w_proj_a, conv_w, conv_b,
           conv_ln_g, conv_ln_b, w_proj_b, b_proj_b, w_out, norm2_g, w_ff1, w_ff2, norm_f_g):
    B, S, D = x.shape
    depth = w_in.shape[0]
    row = lambda p: p.reshape(1, -1)
    x2d = x.reshape(B * S, D)
    for l in range(depth):
        sgub = jnp.broadcast_to(sgu_b[l][:, :, None], (SGU_GROUPS, CHUNK, LANES))
        params = (row(norm1_g[l]), row(b_in[l]), row(sgu_ln_g[l]), row(sgu_ln_b[l]), sgu_w[l], sgub,
                  conv_w[l], row(conv_b[l]), row(conv_ln_g[l]), row(conv_ln_b[l]), row(b_proj_b[l]),
                  row(norm2_g[l]), row(norm_f_g))
        x2d = _block(x2d, l, S // TOKENS, l == depth - 1, params,
                     (w_in, w_proj_a, w_proj_b, w_out, w_ff1, w_ff2))
    return x2d.reshape(B, S, D)
```

```python
import functools
import math

import jax
import jax.numpy as jnp
from jax import lax
from jax.experimental import pallas as pl
from jax.experimental.pallas import tpu as pltpu

CHUNK = 128
SGU_GROUPS = 8
CONV_KERNEL = 31
EPS = 1e-6

LANES = 128
HALO = 32
TOKENS = 256
COLS = 512
CONV_ROWS = 16
STAGE_ROWS, STAGE_COLS = 128, 1024
STAGE_SLOTS = 4
VMEM_CAPACITY = 64 * 1024 * 1024

_GELU_C = math.sqrt(2.0 / math.pi)


def _dot(a, b):
    return jnp.dot(a, b, preferred_element_type=jnp.float32)


def _gelu_tanh_of_twice(half):
    return half + half * jnp.tanh(half * (2.0 * _GELU_C + (8.0 * _GELU_C * 0.044715) * (half * half)))


def _rmsnorm(x, g):
    return x * lax.rsqrt(jnp.mean(x * x, axis=-1, keepdims=True) + EPS) * g


def _layernorm(x, g, b):
    mu = jnp.mean(x, axis=-1, keepdims=True)
    xc = x - mu
    var = jnp.mean(xc * xc, axis=-1, keepdims=True)
    return xc * lax.rsqrt(var + EPS) * g + b


def _stage_weights(layer, pairs, stage_scr, sem):
    chunks = []
    for src, dst, halve_from_col in pairs:
        rows, cols = dst.shape
        assert rows % STAGE_ROWS == 0 and cols % STAGE_COLS == 0
        assert halve_from_col is None or halve_from_col % STAGE_COLS == 0
        for r0 in range(0, rows, STAGE_ROWS):
            for c0 in range(0, cols, STAGE_COLS):
                halve = halve_from_col is not None and c0 >= halve_from_col
                chunks.append((src, dst, r0, c0, halve))

    def copy(j):
        src, _, r0, c0, _ = chunks[j]
        return pltpu.make_async_copy(
            src.at[layer, pl.ds(r0, STAGE_ROWS), pl.ds(c0, STAGE_COLS)],
            stage_scr.at[j % STAGE_SLOTS], sem.at[j % STAGE_SLOTS])

    lookahead = STAGE_SLOTS - 1
    for j in range(min(lookahead, len(chunks))):
        copy(j).start()
    for j, (_, dst, r0, c0, halve) in enumerate(chunks):
        if j + lookahead < len(chunks):
            copy(j + lookahead).start()
        copy(j).wait()
        w = stage_scr[j % STAGE_SLOTS]
        dst[r0:r0 + STAGE_ROWS, c0:c0 + STAGE_COLS] = (0.5 * w if halve else w).astype(jnp.bfloat16)


def _block_kernel(x_ref, n1g_ref, bin_ref, lnag_ref, lnab_ref, sguw_ref, sgub_ref,
                  convw_ref, convb_ref, lncg_ref, lncb_ref, bpb_ref, n2g_ref, nfg_ref,
                  win_hbm, wpa_hbm, wpb_hbm, wout_hbm, w1_hbm, w2_hbm,
                  o_ref,
                  win_scr, wpa_scr, wpb_scr, wout_scr, w1_scr, w2_scr, stage_scr, stage_sem,
                  h_scr, u_scr, v_scr, vn_scr, sg_scr, aext_scr, c_scr, cn_scr, g0_scr, g1_scr, m_scr,
                  x1_scr, h2_scr, f_scr, sguwm_scr,
                  *, layer, final_norm, n_tiles, tiles_per_seq):
    T, D = h_scr.shape
    bf16 = jnp.bfloat16
    step = pl.program_id(0)

    @pl.when(step == 0)
    def _():
        _stage_weights(layer, [(win_hbm, win_scr, 0), (w1_hbm, w1_scr, None),
                               (wpa_hbm, wpa_scr, None), (wpb_hbm, wpb_scr, None),
                               (wout_hbm, wout_scr, 0), (w2_hbm, w2_scr, None)],
                       stage_scr, stage_sem)
        x1_scr[...] = jnp.zeros(x1_scr.shape, x1_scr.dtype)
        h2_scr[...] = jnp.zeros(h2_scr.shape, h2_scr.dtype)
        row = lax.broadcasted_iota(jnp.int32, (CHUNK, CHUNK), 0)
        col = lax.broadcasted_iota(jnp.int32, (CHUNK, CHUNK), 1)
        for g in range(SGU_GROUPS):
            sguwm_scr[g] = jnp.where(row >= col, sguw_ref[g], 0.0).astype(bf16)

    @pl.when(lax.rem(jnp.minimum(step, n_tiles - 1), tiles_per_seq) == 0)
    def _():
        aext_scr[:, 0:HALO, :] = jnp.zeros((D // LANES, HALO, LANES), jnp.float32)

    def ffn_up(cb):
        c0 = cb * COLS
        f = jnp.maximum(_dot(h2_scr[...], w1_scr[:, c0:c0 + COLS]), 0.0)
        f_scr[:, c0:c0 + COLS] = (f * f).astype(bf16)

    def ffn_down(cb):
        c0 = cb * COLS
        o_ref[:, c0:c0 + COLS] = x1_scr[:, c0:c0 + COLS] + _dot(f_scr[...], w2_scr[:, c0:c0 + COLS])

    def ffn_finish():
        if final_norm:
            o_ref[...] = _rmsnorm(o_ref[...], nfg_ref[...])

    def mixer_start():
        h_scr[...] = _rmsnorm(x_ref[...], n1g_ref[...]).astype(bf16)

    def half_in_proj(col0):
        return _dot(h_scr[...], win_scr[:, col0:col0 + COLS]) + 0.5 * bin_ref[:, col0:col0 + COLS]

    def glu_block(cb):
        c0 = cb * COLS
        half_val = half_in_proj(2 * D + c0)
        a = half_val + half_val * jnp.tanh(half_in_proj(3 * D + c0))
        for j in range(COLS // LANES):
            aext_scr[c0 // LANES + j, HALO:HALO + T, :] = a[:, j * LANES:(j + 1) * LANES]

    def conv_block(lb):
        tap0 = HALO - (CONV_KERNEL - 1)
        l0 = lb * LANES
        for rb in range(T // CONV_ROWS):
            r0 = rb * CONV_ROWS
            acc = jnp.zeros((CONV_ROWS, LANES), jnp.float32) + convb_ref[:, l0:l0 + LANES]
            for k in range(CONV_KERNEL):
                a_k = aext_scr[lb, pl.ds(r0 + tap0 + k, CONV_ROWS, stride=1), :]
                acc = acc + convw_ref[k:k + 1, l0:l0 + LANES] * a_k
            c_scr[r0:r0 + CONV_ROWS, l0:l0 + LANES] = acc

    def gelu_block(dst_scr, seg, cb):
        c0 = cb * COLS
        dst_scr[:, c0:c0 + COLS] = _gelu_tanh_of_twice(half_in_proj(seg * D + c0))

    def gate_block(dst_scr, seg, cb):
        c0 = cb * COLS
        dst_scr[:, c0:c0 + COLS] = jnp.tanh(half_in_proj(seg * D + c0))

    def halo_keep():
        aext_scr[:, 0:HALO, :] = aext_scr[:, T:T + HALO, :]

    def spatial_gating():
        vn_scr[...] = _layernorm(v_scr[...], lnag_ref[...], lnab_ref[...]).astype(bf16)
        n_chunks = T // CHUNK
        for g in range(SGU_GROUPS):
            c0 = g * LANES
            vg = jnp.concatenate(
                [vn_scr[n * CHUNK:(n + 1) * CHUNK, c0:c0 + LANES] for n in range(n_chunks)], axis=1)
            mixed = _dot(sguwm_scr[g], vg)
            for n in range(n_chunks):
                r0 = n * CHUNK
                sg_scr[r0:r0 + CHUNK, c0:c0 + LANES] = (
                    u_scr[r0:r0 + CHUNK, c0:c0 + LANES]
                    * (mixed[:, n * LANES:(n + 1) * LANES] + sgub_ref[g])).astype(bf16)

    def conv_tail():
        half_cn = _layernorm(c_scr[...], 0.5 * lncg_ref[...], 0.5 * lncb_ref[...])
        cn_scr[...] = (half_cn + half_cn * jnp.tanh(half_cn)).astype(bf16)

    def merge():
        y_a = _dot(sg_scr[...], wpa_scr[...])
        y_b = _dot(cn_scr[...], wpb_scr[...]) + bpb_ref[...]
        m_scr[...] = ((y_a + y_b) + (g0_scr[...] * y_a + g1_scr[...] * y_b)).astype(bf16)

    def mixer_finish():
        x1 = x_ref[...] + _dot(m_scr[...], wout_scr[...])
        x1_scr[...] = x1
        h2_scr[...] = _rmsnorm(x1, n2g_ref[...]).astype(bf16)

    n_in = D // COLS
    ffn_work = ([functools.partial(ffn_up, cb) for cb in range(f_scr.shape[1] // COLS)]
                + [functools.partial(ffn_down, cb) for cb in range(n_in)])

    def next_ffn():
        if ffn_work:
            ffn_work.pop(0)()

    next_ffn()
    mixer_start()
    for cb in range(n_in):
        next_ffn()
        glu_block(cb)
    for lb in range(D // LANES):
        conv_block(lb)
        next_ffn()
    while ffn_work:
        next_ffn()
    halo_keep()
    ffn_finish()
    for cb in range(n_in):
        gelu_block(u_scr, 0, cb)
    for cb in range(n_in):
        if cb == n_in - 1:
            conv_tail()
        gelu_block(v_scr, 1, cb)
    gate_block(g0_scr, 4, 0)
    spatial_gating()
    for cb in range(1, n_in):
        gate_block(g0_scr, 4, cb)
    for cb in range(n_in):
        gate_block(g1_scr, 5, cb)
    merge()
    mixer_finish()


def _resident(shape):
    zeros = (0,) * len(shape)
    return pl.BlockSpec(shape, lambda *_: zeros, pipeline_mode=pl.Buffered(1))


def _nbytes(shape, dtype):
    return math.prod(shape) * jnp.dtype(dtype).itemsize


def _vmem_limit(resident, streamed, scratch, temporaries):
    total = sum(_nbytes(*b) for b in resident) + 2 * sum(_nbytes(*b) for b in streamed)
    total += sum(_nbytes(*b) for b in scratch) + sum(_nbytes(*b) for b in temporaries)
    assert total <= VMEM_CAPACITY, total
    return total


def _block(x2d, layer, tiles_per_seq, final_norm, params, weights):
    M, D = x2d.shape
    F = weights[4].shape[2]
    T = TOKENS
    assert M % T == 0 and T % CHUNK == 0 and D % COLS == 0 and F % COLS == 0 and T % CONV_ROWS == 0
    n_tiles = M // T
    f32, bf16 = jnp.float32, jnp.bfloat16
    weight_buffers = [(w.shape[1:], bf16) for w in weights] + [((STAGE_SLOTS, STAGE_ROWS, STAGE_COLS), f32)]
    scratch = [
        ((T, D), bf16),
        ((T, D), f32),
        ((T, D), f32),
        ((T, D), bf16),
        ((T, D), bf16),
        ((D // LANES, HALO + T, LANES), f32),
        ((T, D), f32),
        ((T, D), bf16),
        ((T, D), f32),
        ((T, D), f32),
        ((T, D), bf16),
        ((T, D), f32),
        ((T, D), bf16),
        ((T, F), bf16),
        ((SGU_GROUPS, CHUNK, CHUNK), bf16),
    ]
    limit = _vmem_limit(
        resident=[(p.shape, p.dtype) for p in params],
        streamed=[((T, D), f32), ((T, D), f32)],
        scratch=weight_buffers + scratch,
        temporaries=[((T, D), f32)] * 2)
    return pl.pallas_call(
        functools.partial(_block_kernel, layer=layer, final_norm=final_norm, n_tiles=n_tiles,
                          tiles_per_seq=tiles_per_seq),
        grid=(n_tiles + 1,),
        in_specs=[pl.BlockSpec((T, D), lambda i: (jnp.minimum(i, n_tiles - 1), 0))]
                 + [_resident(p.shape) for p in params]
                 + [pl.BlockSpec(memory_space=pl.ANY)] * len(weights),
        out_specs=pl.BlockSpec((T, D), lambda i: (jnp.maximum(i - 1, 0), 0)),
        out_shape=jax.ShapeDtypeStruct(x2d.shape, x2d.dtype),
        scratch_shapes=[pltpu.VMEM(shape, dtype) for shape, dtype in weight_buffers]
                       + [pltpu.SemaphoreType.DMA((STAGE_SLOTS,))]
                       + [pltpu.VMEM(shape, dtype) for shape, dtype in scratch],
        compiler_params=pltpu.CompilerParams(
            dimension_semantics=("arbitrary",), vmem_limit_bytes=limit),
        name="block",
    )(x2d, *params, *weights)


def kernel(x, norm1_g, w_in, b_in, sgu_ln_g, sgu_ln_b, sgu_w, sgu_b, w_proj_a, conv_w, conv_b,
           conv_ln_g, conv_ln_b, w_proj_b, b_proj_b, w_out, norm2_g, w_ff1, w_ff2, norm_f_g):
    B, S, D = x.shape
    depth = w_in.shape[0]
    row = lambda p: p.reshape(1, -1)
    x2d = x.reshape(B * S, D)
    for l in range(depth):
        sgub = jnp.broadcast_to(sgu_b[l][:, :, None], (SGU_GROUPS, CHUNK, LANES))
        params = (row(norm1_g[l]), row(b_in[l]), row(sgu_ln_g[l]), row(sgu_ln_b[l]), sgu_w[l], sgub,
                  conv_w[l], row(conv_b[l]), row(conv_ln_g[l]), row(conv_ln_b[l]), row(b_proj_b[l]),
                  row(norm2_g[l]), row(norm_f_g))
        x2d = _block(x2d, l, S // TOKENS, l == depth - 1, params,
                     (w_in, w_proj_a, w_proj_b, w_out, w_ff1, w_ff2))
    return x2d.reshape(B, S, D)
```

```python
import functools
import math

import jax
import jax.numpy as jnp
from jax import lax
from jax.experimental import pallas as pl
from jax.experimental.pallas import tpu as pltpu

CHUNK = 128
SGU_GROUPS = 8
CONV_KERNEL = 31
EPS = 1e-6

LANES = 128
HALO = 32
TOKENS = 256
COLS = 512
CONV_ROWS = 16
STAGE_ROWS, STAGE_COLS = 64, 1024
STAGE_SLOTS = 8
VMEM_CAPACITY = 64 * 1024 * 1024

_GELU_C = math.sqrt(2.0 / math.pi)


def _dot(a, b):
    return jnp.dot(a, b, preferred_element_type=jnp.float32)


def _gelu_tanh_of_twice(half):
    return half + half * jnp.tanh(half * (2.0 * _GELU_C + (8.0 * _GELU_C * 0.044715) * (half * half)))


def _rmsnorm(x, g):
    return x * lax.rsqrt(jnp.mean(x * x, axis=-1, keepdims=True) + EPS) * g


def _layernorm(x, g, b):
    mu = jnp.mean(x, axis=-1, keepdims=True)
    xc = x - mu
    var = jnp.mean(xc * xc, axis=-1, keepdims=True)
    return xc * lax.rsqrt(var + EPS) * g + b


def _stage_weights(layer, pairs, stage_scr, sem):
    chunks = []
    for src, dst, halve_from_col in pairs:
        rows, cols = dst.shape
        assert rows % STAGE_ROWS == 0 and cols % STAGE_COLS == 0
        assert halve_from_col is None or halve_from_col % STAGE_COLS == 0
        for r0 in range(0, rows, STAGE_ROWS):
            for c0 in range(0, cols, STAGE_COLS):
                halve = halve_from_col is not None and c0 >= halve_from_col
                chunks.append((src, dst, r0, c0, halve))

    def copy(j):
        src, _, r0, c0, _ = chunks[j]
        return pltpu.make_async_copy(
            src.at[layer, pl.ds(r0, STAGE_ROWS), pl.ds(c0, STAGE_COLS)],
            stage_scr.at[j % STAGE_SLOTS], sem.at[j % STAGE_SLOTS])

    lookahead = STAGE_SLOTS - 1
    for j in range(min(lookahead, len(chunks))):
        copy(j).start()
    for j, (_, dst, r0, c0, halve) in enumerate(chunks):
        if j + lookahead < len(chunks):
            copy(j + lookahead).start()
        copy(j).wait()
        w = stage_scr[j % STAGE_SLOTS]
        dst[r0:r0 + STAGE_ROWS, c0:c0 + STAGE_COLS] = (0.5 * w if halve else w).astype(jnp.bfloat16)


def _block_kernel(x_ref, n1g_ref, bin_ref, lnag_ref, lnab_ref, sguw_ref, sgub_ref,
                  convw_ref, convb_ref, lncg_ref, lncb_ref, bpb_ref, n2g_ref, nfg_ref,
                  win_hbm, wpa_hbm, wpb_hbm, wout_hbm, w1_hbm, w2_hbm,
                  o_ref,
                  win_scr, wpa_scr, wpb_scr, wout_scr, w1_scr, w2_scr, stage_scr, stage_sem,
                  h_scr, u_scr, v_scr, vn_scr, sg_scr, aext_scr, c_scr, cn_scr, g0_scr, g1_scr, m_scr,
                  x1_scr, h2_scr, f_scr, sguwm_scr,
                  *, layer, final_norm, n_tiles, tiles_per_seq):
    T, D = h_scr.shape
    bf16 = jnp.bfloat16
    step = pl.program_id(0)

    @pl.when(step == 0)
    def _():
        _stage_weights(layer, [(win_hbm, win_scr, 0), (w1_hbm, w1_scr, None),
                               (wpa_hbm, wpa_scr, None), (wpb_hbm, wpb_scr, None),
                               (wout_hbm, wout_scr, 0), (w2_hbm, w2_scr, None)],
                       stage_scr, stage_sem)
        x1_scr[...] = jnp.zeros(x1_scr.shape, x1_scr.dtype)
        h2_scr[...] = jnp.zeros(h2_scr.shape, h2_scr.dtype)
        row = lax.broadcasted_iota(jnp.int32, (CHUNK, CHUNK), 0)
        col = lax.broadcasted_iota(jnp.int32, (CHUNK, CHUNK), 1)
        for g in range(SGU_GROUPS):
            sguwm_scr[g] = jnp.where(row >= col, sguw_ref[g], 0.0).astype(bf16)

    @pl.when(lax.rem(jnp.minimum(step, n_tiles - 1), tiles_per_seq) == 0)
    def _():
        aext_scr[:, 0:HALO, :] = jnp.zeros((D // LANES, HALO, LANES), jnp.float32)

    def ffn_up(cb):
        c0 = cb * COLS
        f = jnp.maximum(_dot(h2_scr[...], w1_scr[:, c0:c0 + COLS]), 0.0)
        f_scr[:, c0:c0 + COLS] = (f * f).astype(bf16)

    def ffn_down(cb):
        c0 = cb * COLS
        o_ref[:, c0:c0 + COLS] = x1_scr[:, c0:c0 + COLS] + _dot(f_scr[...], w2_scr[:, c0:c0 + COLS])

    def ffn_finish():
        if final_norm:
            o_ref[...] = _rmsnorm(o_ref[...], nfg_ref[...])

    def mixer_start():
        h_scr[...] = _rmsnorm(x_ref[...], n1g_ref[...]).astype(bf16)

    def half_in_proj(col0):
        return _dot(h_scr[...], win_scr[:, col0:col0 + COLS]) + 0.5 * bin_ref[:, col0:col0 + COLS]

    def glu_block(cb):
        c0 = cb * COLS
        half_val = half_in_proj(2 * D + c0)
        a = half_val + half_val * jnp.tanh(half_in_proj(3 * D + c0))
        for j in range(COLS // LANES):
            aext_scr[c0 // LANES + j, HALO:HALO + T, :] = a[:, j * LANES:(j + 1) * LANES]

    def conv_block(lb):
        tap0 = HALO - (CONV_KERNEL - 1)
        l0 = lb * LANES
        for rb in range(T // CONV_ROWS):
            r0 = rb * CONV_ROWS
            acc = jnp.zeros((CONV_ROWS, LANES), jnp.float32) + convb_ref[:, l0:l0 + LANES]
            for k in range(CONV_KERNEL):
                a_k = aext_scr[lb, pl.ds(r0 + tap0 + k, CONV_ROWS, stride=1), :]
                acc = acc + convw_ref[k:k + 1, l0:l0 + LANES] * a_k
            c_scr[r0:r0 + CONV_ROWS, l0:l0 + LANES] = acc

    def gelu_block(dst_scr, seg, cb):
        c0 = cb * COLS
        dst_scr[:, c0:c0 + COLS] = _gelu_tanh_of_twice(half_in_proj(seg * D + c0))

    def gate_block(dst_scr, seg, cb):
        c0 = cb * COLS
        dst_scr[:, c0:c0 + COLS] = jnp.tanh(half_in_proj(seg * D + c0))

    def halo_keep():
        aext_scr[:, 0:HALO, :] = aext_scr[:, T:T + HALO, :]

    def spatial_gating():
        vn_scr[...] = _layernorm(v_scr[...], lnag_ref[...], lnab_ref[...]).astype(bf16)
        n_chunks = T // CHUNK
        for g in range(SGU_GROUPS):
            c0 = g * LANES
            vg = jnp.concatenate(
                [vn_scr[n * CHUNK:(n + 1) * CHUNK, c0:c0 + LANES] for n in range(n_chunks)], axis=1)
            mixed = _dot(sguwm_scr[g], vg)
            for n in range(n_chunks):
                r0 = n * CHUNK
                sg_scr[r0:r0 + CHUNK, c0:c0 + LANES] = (
                    u_scr[r0:r0 + CHUNK, c0:c0 + LANES]
                    * (mixed[:, n * LANES:(n + 1) * LANES] + sgub_ref[g])).astype(bf16)

    def conv_tail():
        half_cn = _layernorm(c_scr[...], 0.5 * lncg_ref[...], 0.5 * lncb_ref[...])
        cn_scr[...] = (half_cn + half_cn * jnp.tanh(half_cn)).astype(bf16)

    def merge():
        y_a = _dot(sg_scr[...], wpa_scr[...])
        y_b = _dot(cn_scr[...], wpb_scr[...]) + bpb_ref[...]
        m_scr[...] = ((y_a + y_b) + (g0_scr[...] * y_a + g1_scr[...] * y_b)).astype(bf16)

    def mixer_finish():
        x1 = x_ref[...] + _dot(m_scr[...], wout_scr[...])
        x1_scr[...] = x1
        h2_scr[...] = _rmsnorm(x1, n2g_ref[...]).astype(bf16)

    n_in = D // COLS
    ffn_work = ([functools.partial(ffn_up, cb) for cb in range(f_scr.shape[1] // COLS)]
                + [functools.partial(ffn_down, cb) for cb in range(n_in)])

    def next_ffn():
        if ffn_work:
            ffn_work.pop(0)()

    next_ffn()
    mixer_start()
    for cb in range(n_in):
        next_ffn()
        glu_block(cb)
    for lb in range(D // LANES):
        conv_block(lb)
        next_ffn()
    while ffn_work:
        next_ffn()
    halo_keep()
    ffn_finish()
    for cb in range(n_in):
        gelu_block(u_scr, 0, cb)
    for cb in range(n_in):
        if cb == n_in - 1:
            conv_tail()
        gelu_block(v_scr, 1, cb)
    gate_block(g0_scr, 4, 0)
    spatial_gating()
    for cb in range(1, n_in):
        gate_block(g0_scr, 4, cb)
    for cb in range(n_in):
        gate_block(g1_scr, 5, cb)
    merge()
    mixer_finish()


def _resident(shape):
    zeros = (0,) * len(shape)
    return pl.BlockSpec(shape, lambda *_: zeros, pipeline_mode=pl.Buffered(1))


def _nbytes(shape, dtype):
    return math.prod(shape) * jnp.dtype(dtype).itemsize


def _vmem_limit(resident, streamed, scratch, temporaries):
    total = sum(_nbytes(*b) for b in resident) + 2 * sum(_nbytes(*b) for b in streamed)
    total += sum(_nbytes(*b) for b in scratch) + sum(_nbytes(*b) for b in temporaries)
    assert total <= VMEM_CAPACITY, total
    return total


def _block(x2d, layer, tiles_per_seq, final_norm, params, weights):
    M, D = x2d.shape
    F = weights[4].shape[2]
    T = TOKENS
    assert M % T == 0 and T % CHUNK == 0 and D % COLS == 0 and F % COLS == 0 and T % CONV_ROWS == 0
    n_tiles = M // T
    f32, bf16 = jnp.float32, jnp.bfloat16
    weight_buffers = [(w.shape[1:], bf16) for w in weights] + [((STAGE_SLOTS, STAGE_ROWS, STAGE_COLS), f32)]
    scratch = [
        ((T, D), bf16),
        ((T, D), f32),
        ((T, D), f32),
        ((T, D), bf16),
        ((T, D), bf16),
        ((D // LANES, HALO + T, LANES), f32),
        ((T, D), f32),
        ((T, D), bf16),
        ((T, D), f32),
        ((T, D), f32),
        ((T, D), bf16),
        ((T, D), f32),
        ((T, D), bf16),
        ((T, F), bf16),
        ((SGU_GROUPS, CHUNK, CHUNK), bf16),
    ]
    limit = _vmem_limit(
        resident=[(p.shape, p.dtype) for p in params],
        streamed=[((T, D), f32), ((T, D), f32)],
        scratch=weight_buffers + scratch,
        temporaries=[((T, D), f32)] * 2)
    return pl.pallas_call(
        functools.partial(_block_kernel, layer=layer, final_norm=final_norm, n_tiles=n_tiles,
                          tiles_per_seq=tiles_per_seq),
        grid=(n_tiles + 1,),
        in_specs=[pl.BlockSpec((T, D), lambda i: (jnp.minimum(i, n_tiles - 1), 0))]
                 + [_resident(p.shape) for p in params]
                 + [pl.BlockSpec(memory_space=pl.ANY)] * len(weights),
        out_specs=pl.BlockSpec((T, D), lambda i: (jnp.maximum(i - 1, 0), 0)),
        out_shape=jax.ShapeDtypeStruct(x2d.shape, x2d.dtype),
        scratch_shapes=[pltpu.VMEM(shape, dtype) for shape, dtype in weight_buffers]
                       + [pltpu.SemaphoreType.DMA((STAGE_SLOTS,))]
                       + [pltpu.VMEM(shape, dtype) for shape, dtype in scratch],
        compiler_params=pltpu.CompilerParams(
            dimension_semantics=("arbitrary",), vmem_limit_bytes=limit),
        name="block",
    )(x2d, *params, *weights)


def kernel(x, norm1_g, w_in, b_in, sgu_ln_g, sgu_ln_b, sgu_w, sgu_b, w_proj_a, conv_w, conv_b,
           conv_ln_g, conv_ln_b, w_proj_b, b_proj_b, w_out, norm2_g, w_ff1, w_ff2, norm_f_g):
    B, S, D = x.shape
    depth = w_in.shape[0]
    row = lambda p: p.reshape(1, -1)
    x2d = x.reshape(B * S, D)
    for l in range(depth):
        sgub = jnp.broadcast_to(sgu_b[l][:, :, None], (SGU_GROUPS, CHUNK, LANES))
        params = (row(norm1_g[l]), row(b_in[l]), row(sgu_ln_g[l]), row(sgu_ln_b[l]), sgu_w[l], sgub,
                  conv_w[l], row(conv_b[l]), row(conv_ln_g[l]), row(conv_ln_b[l]), row(b_proj_b[l]),
                  row(norm2_g[l]), row(norm_f_g))
        x2d = _block(x2d, l, S // TOKENS, l == depth - 1, params,
                     (w_in, w_proj_a, w_proj_b, w_out, w_ff1, w_ff2))
    return x2d.reshape(B, S, D)
```

```python
import functools
import math

import jax
import jax.numpy as jnp
from jax import lax
from jax.experimental import pallas as pl
from jax.experimental.pallas import tpu as pltpu

CHUNK = 128
SGU_GROUPS = 8
CONV_KERNEL = 31
EPS = 1e-6

LANES = 128
HALO = 32
TOKENS = 256
COLS = 512
CONV_ROWS = 16
STAGE_ROWS, STAGE_COLS = 32, 1024
STAGE_SLOTS = 16
VMEM_CAPACITY = 64 * 1024 * 1024

_GELU_C = math.sqrt(2.0 / math.pi)


def _dot(a, b):
    return jnp.dot(a, b, preferred_element_type=jnp.float32)


def _gelu_tanh_of_twice(half):
    return half + half * jnp.tanh(half * (2.0 * _GELU_C + (8.0 * _GELU_C * 0.044715) * (half * half)))


def _rmsnorm(x, g):
    return x * lax.rsqrt(jnp.mean(x * x, axis=-1, keepdims=True) + EPS) * g


def _layernorm(x, g, b):
    mu = jnp.mean(x, axis=-1, keepdims=True)
    xc = x - mu
    var = jnp.mean(xc * xc, axis=-1, keepdims=True)
    return xc * lax.rsqrt(var + EPS) * g + b


def _stage_weights(layer, pairs, stage_scr, sem):
    chunks = []
    for src, dst, halve_from_col in pairs:
        rows, cols = dst.shape
        assert rows % STAGE_ROWS == 0 and cols % STAGE_COLS == 0
        assert halve_from_col is None or halve_from_col % STAGE_COLS == 0
        for r0 in range(0, rows, STAGE_ROWS):
            for c0 in range(0, cols, STAGE_COLS):
                halve = halve_from_col is not None and c0 >= halve_from_col
                chunks.append((src, dst, r0, c0, halve))

    def copy(j):
        src, _, r0, c0, _ = chunks[j]
        return pltpu.make_async_copy(
            src.at[layer, pl.ds(r0, STAGE_ROWS), pl.ds(c0, STAGE_COLS)],
            stage_scr.at[j % STAGE_SLOTS], sem.at[j % STAGE_SLOTS])

    lookahead = STAGE_SLOTS - 1
    for j in range(min(lookahead, len(chunks))):
        copy(j).start()
    for j, (_, dst, r0, c0, halve) in enumerate(chunks):
        if j + lookahead < len(chunks):
            copy(j + lookahead).start()
        copy(j).wait()
        w = stage_scr[j % STAGE_SLOTS]
        dst[r0:r0 + STAGE_ROWS, c0:c0 + STAGE_COLS] = (0.5 * w if halve else w).astype(jnp.bfloat16)


def _block_kernel(x_ref, n1g_ref, bin_ref, lnag_ref, lnab_ref, sguw_ref, sgub_ref,
                  convw_ref, convb_ref, lncg_ref, lncb_ref, bpb_ref, n2g_ref, nfg_ref,
                  win_hbm, wpa_hbm, wpb_hbm, wout_hbm, w1_hbm, w2_hbm,
                  o_ref,
                  win_scr, wpa_scr, wpb_scr, wout_scr, w1_scr, w2_scr, stage_scr, stage_sem,
                  h_scr, u_scr, v_scr, vn_scr, sg_scr, aext_scr, c_scr, cn_scr, g0_scr, g1_scr, m_scr,
                  x1_scr, h2_scr, f_scr, sguwm_scr,
                  *, layer, final_norm, n_tiles, tiles_per_seq):
    T, D = h_scr.shape
    bf16 = jnp.bfloat16
    step = pl.program_id(0)

    @pl.when(step == 0)
    def _():
        _stage_weights(layer, [(win_hbm, win_scr, 0), (w1_hbm, w1_scr, None),
                               (wpa_hbm, wpa_scr, None), (wpb_hbm, wpb_scr, None),
                               (wout_hbm, wout_scr, 0), (w2_hbm, w2_scr, None)],
                       stage_scr, stage_sem)
        x1_scr[...] = jnp.zeros(x1_scr.shape, x1_scr.dtype)
        h2_scr[...] = jnp.zeros(h2_scr.shape, h2_scr.dtype)
        row = lax.broadcasted_iota(jnp.int32, (CHUNK, CHUNK), 0)
        col = lax.broadcasted_iota(jnp.int32, (CHUNK, CHUNK), 1)
        for g in range(SGU_GROUPS):
            sguwm_scr[g] = jnp.where(row >= col, sguw_ref[g], 0.0).astype(bf16)

    @pl.when(lax.rem(jnp.minimum(step, n_tiles - 1), tiles_per_seq) == 0)
    def _():
        aext_scr[:, 0:HALO, :] = jnp.zeros((D // LANES, HALO, LANES), jnp.float32)

    def ffn_up(cb):
        c0 = cb * COLS
        f = jnp.maximum(_dot(h2_scr[...], w1_scr[:, c0:c0 + COLS]), 0.0)
        f_scr[:, c0:c0 + COLS] = (f * f).astype(bf16)

    def ffn_down(cb):
        c0 = cb * COLS
        o_ref[:, c0:c0 + COLS] = x1_scr[:, c0:c0 + COLS] + _dot(f_scr[...], w2_scr[:, c0:c0 + COLS])

    def ffn_finish():
        if final_norm:
            o_ref[...] = _rmsnorm(o_ref[...], nfg_ref[...])

    def mixer_start():
        h_scr[...] = _rmsnorm(x_ref[...], n1g_ref[...]).astype(bf16)

    def half_in_proj(col0):
        return _dot(h_scr[...], win_scr[:, col0:col0 + COLS]) + 0.5 * bin_ref[:, col0:col0 + COLS]

    def glu_block(cb):
        c0 = cb * COLS
        half_val = half_in_proj(2 * D + c0)
        a = half_val + half_val * jnp.tanh(half_in_proj(3 * D + c0))
        for j in range(COLS // LANES):
            aext_scr[c0 // LANES + j, HALO:HALO + T, :] = a[:, j * LANES:(j + 1) * LANES]

    def conv_block(lb):
        tap0 = HALO - (CONV_KERNEL - 1)
        l0 = lb * LANES
        for rb in range(T // CONV_ROWS):
            r0 = rb * CONV_ROWS
            acc = jnp.zeros((CONV_ROWS, LANES), jnp.float32) + convb_ref[:, l0:l0 + LANES]
            for k in range(CONV_KERNEL):
                a_k = aext_scr[lb, pl.ds(r0 + tap0 + k, CONV_ROWS, stride=1), :]
                acc = acc + convw_ref[k:k + 1, l0:l0 + LANES] * a_k
            c_scr[r0:r0 + CONV_ROWS, l0:l0 + LANES] = acc

    def gelu_block(dst_scr, seg, cb):
        c0 = cb * COLS
        dst_scr[:, c0:c0 + COLS] = _gelu_tanh_of_twice(half_in_proj(seg * D + c0))

    def gate_block(dst_scr, seg, cb):
        c0 = cb * COLS
        dst_scr[:, c0:c0 + COLS] = jnp.tanh(half_in_proj(seg * D + c0))

    def halo_keep():
        aext_scr[:, 0:HALO, :] = aext_scr[:, T:T + HALO, :]

    def spatial_gating():
        vn_scr[...] = _layernorm(v_scr[...], lnag_ref[...], lnab_ref[...]).astype(bf16)
        n_chunks = T // CHUNK
        for g in range(SGU_GROUPS):
            c0 = g * LANES
            vg = jnp.concatenate(
                [vn_scr[n * CHUNK:(n + 1) * CHUNK, c0:c0 + LANES] for n in range(n_chunks)], axis=1)
            mixed = _dot(sguwm_scr[g], vg)
            for n in range(n_chunks):
                r0 = n * CHUNK
                sg_scr[r0:r0 + CHUNK, c0:c0 + LANES] = (
                    u_scr[r0:r0 + CHUNK, c0:c0 + LANES]
                    * (mixed[:, n * LANES:(n + 1) * LANES] + sgub_ref[g])).astype(bf16)

    def conv_tail():
        half_cn = _layernorm(c_scr[...], 0.5 * lncg_ref[...], 0.5 * lncb_ref[...])
        cn_scr[...] = (half_cn + half_cn * jnp.tanh(half_cn)).astype(bf16)

    def merge():
        y_a = _dot(sg_scr[...], wpa_scr[...])
        y_b = _dot(cn_scr[...], wpb_scr[...]) + bpb_ref[...]
        m_scr[...] = ((y_a + y_b) + (g0_scr[...] * y_a + g1_scr[...] * y_b)).astype(bf16)

    def mixer_finish():
        x1 = x_ref[...] + _dot(m_scr[...], wout_scr[...])
        x1_scr[...] = x1
        h2_scr[...] = _rmsnorm(x1, n2g_ref[...]).astype(bf16)

    n_in = D // COLS
    ffn_work = ([functools.partial(ffn_up, cb) for cb in range(f_scr.shape[1] // COLS)]
                + [functools.partial(ffn_down, cb) for cb in range(n_in)])

    def next_ffn():
        if ffn_work:
            ffn_work.pop(0)()

    next_ffn()
    mixer_start()
    for cb in range(n_in):
        next_ffn()
        glu_block(cb)
    for lb in range(D // LANES):
        conv_block(lb)
        next_ffn()
    while ffn_work:
        next_ffn()
    halo_keep()
    ffn_finish()
    for cb in range(n_in):
        gelu_block(u_scr, 0, cb)
    for cb in range(n_in):
        if cb == n_in - 1:
            conv_tail()
        gelu_block(v_scr, 1, cb)
    gate_block(g0_scr, 4, 0)
    spatial_gating()
    for cb in range(1, n_in):
        gate_block(g0_scr, 4, cb)
    for cb in range(n_in):
        gate_block(g1_scr, 5, cb)
    merge()
    mixer_finish()


def _resident(shape):
    zeros = (0,) * len(shape)
    return pl.BlockSpec(shape, lambda *_: zeros, pipeline_mode=pl.Buffered(1))


def _nbytes(shape, dtype):
    return math.prod(shape) * jnp.dtype(dtype).itemsize


def _vmem_limit(resident, streamed, scratch, temporaries):
    total = sum(_nbytes(*b) for b in resident) + 2 * sum(_nbytes(*b) for b in streamed)
    total += sum(_nbytes(*b) for b in scratch) + sum(_nbytes(*b) for b in temporaries)
    assert total <= VMEM_CAPACITY, total
    return total


def _block(x2d, layer, tiles_per_seq, final_norm, params, weights):
    M, D = x2d.shape
    F = weights[4].shape[2]
    T = TOKENS
    assert M % T == 0 and T % CHUNK == 0 and D % COLS == 0 and F % COLS == 0 and T % CONV_ROWS == 0
    n_tiles = M // T
    f32, bf16 = jnp.float32, jnp.bfloat16
    weight_buffers = [(w.shape[1:], bf16) for w in weights] + [((STAGE_SLOTS, STAGE_ROWS, STAGE_COLS), f32)]
    scratch = [
        ((T, D), bf16),
        ((T, D), f32),
        ((T, D), f32),
        ((T, D), bf16),
        ((T, D), bf16),
        ((D // LANES, HALO + T, LANES), f32),
        ((T, D), f32),
        ((T, D), bf16),
        ((T, D), f32),
        ((T, D), f32),
        ((T, D), bf16),
        ((T, D), f32),
        ((T, D), bf16),
        ((T, F), bf16),
        ((SGU_GROUPS, CHUNK, CHUNK), bf16),
    ]
    limit = _vmem_limit(
        resident=[(p.shape, p.dtype) for p in params],
        streamed=[((T, D), f32), ((T, D), f32)],
        scratch=weight_buffers + scratch,
        temporaries=[((T, D), f32)] * 2)
    return pl.pallas_call(
        functools.partial(_block_kernel, layer=layer, final_norm=final_norm, n_tiles=n_tiles,
                          tiles_per_seq=tiles_per_seq),
        grid=(n_tiles + 1,),
        in_specs=[pl.BlockSpec((T, D), lambda i: (jnp.minimum(i, n_tiles - 1), 0))]
                 + [_resident(p.shape) for p in params]
                 + [pl.BlockSpec(memory_space=pl.ANY)] * len(weights),
        out_specs=pl.BlockSpec((T, D), lambda i: (jnp.maximum(i - 1, 0), 0)),
        out_shape=jax.ShapeDtypeStruct(x2d.shape, x2d.dtype),
        scratch_shapes=[pltpu.VMEM(shape, dtype) for shape, dtype in weight_buffers]
                       + [pltpu.SemaphoreType.DMA((STAGE_SLOTS,))]
                       + [pltpu.VMEM(shape, dtype) for shape, dtype in scratch],
        compiler_params=pltpu.CompilerParams(
            dimension_semantics=("arbitrary",), vmem_limit_bytes=limit),
        name="block",
    )(x2d, *params, *weights)


def kernel(x, norm1_g, w_in, b_in, sgu_ln_g, sgu_ln_b, sgu_w, sgu_b, w_proj_a, conv_w, conv_b,
           conv_ln_g, conv_ln_b, w_proj_b, b_proj_b, w_out, norm2_g, w_ff1, w_ff2, norm_f_g):
    B, S, D = x.shape
    depth = w_in.shape[0]
    row = lambda p: p.reshape(1, -1)
    x2d = x.reshape(B * S, D)
    for l in range(depth):
        sgub = jnp.broadcast_to(sgu_b[l][:, :, None], (SGU_GROUPS, CHUNK, LANES))
        params = (row(norm1_g[l]), row(b_in[l]), row(sgu_ln_g[l]), row(sgu_ln_b[l]), sgu_w[l], sgub,
                  conv_w[l], row(conv_b[l]), row(conv_ln_g[l]), row(conv_ln_b[l]), row(b_proj_b[l]),
                  row(norm2_g[l]), row(norm_f_g))
        x2d = _block(x2d, l, S // TOKENS, l == depth - 1, params,
                     (w_in, w_proj_a, w_proj_b, w_out, w_ff1, w_ff2))
    return x2d.reshape(B, S, D)
```

```python
import functools
import math

import jax
import jax.numpy as jnp
from jax import lax
from jax.experimental import pallas as pl
from jax.experimental.pallas import tpu as pltpu

CHUNK = 128
SGU_GROUPS = 8
CONV_KERNEL = 31
EPS = 1e-6

LANES = 128
HALO = 32
TOKENS = 256
COLS = 1024
CONV_ROWS = 16
STAGE_ROWS, STAGE_COLS = 64, 1024
STAGE_SLOTS = 8
VMEM_CAPACITY = 64 * 1024 * 1024

_GELU_C = math.sqrt(2.0 / math.pi)


def _dot(a, b):
    return jnp.dot(a, b, preferred_element_type=jnp.float32)


def _gelu_tanh_of_twice(half):
    return half + half * jnp.tanh(half * (2.0 * _GELU_C + (8.0 * _GELU_C * 0.044715) * (half * half)))


def _rmsnorm(x, g):
    return x * lax.rsqrt(jnp.mean(x * x, axis=-1, keepdims=True) + EPS) * g


def _layernorm(x, g, b):
    mu = jnp.mean(x, axis=-1, keepdims=True)
    xc = x - mu
    var = jnp.mean(xc * xc, axis=-1, keepdims=True)
    return xc * lax.rsqrt(var + EPS) * g + b


def _stage_weights(layer, pairs, stage_scr, sem):
    chunks = []
    for src, dst, halve_from_col in pairs:
        rows, cols = dst.shape
        assert rows % STAGE_ROWS == 0 and cols % STAGE_COLS == 0
        assert halve_from_col is None or halve_from_col % STAGE_COLS == 0
        for r0 in range(0, rows, STAGE_ROWS):
            for c0 in range(0, cols, STAGE_COLS):
                halve = halve_from_col is not None and c0 >= halve_from_col
                chunks.append((src, dst, r0, c0, halve))

    def copy(j):
        src, _, r0, c0, _ = chunks[j]
        return pltpu.make_async_copy(
            src.at[layer, pl.ds(r0, STAGE_ROWS), pl.ds(c0, STAGE_COLS)],
            stage_scr.at[j % STAGE_SLOTS], sem.at[j % STAGE_SLOTS])

    lookahead = STAGE_SLOTS - 1
    for j in range(min(lookahead, len(chunks))):
        copy(j).start()
    for j, (_, dst, r0, c0, halve) in enumerate(chunks):
        if j + lookahead < len(chunks):
            copy(j + lookahead).start()
        copy(j).wait()
        w = stage_scr[j % STAGE_SLOTS]
        dst[r0:r0 + STAGE_ROWS, c0:c0 + STAGE_COLS] = (0.5 * w if halve else w).astype(jnp.bfloat16)


def _block_kernel(x_ref, n1g_ref, bin_ref, lnag_ref, lnab_ref, sguw_ref, sgub_ref,
                  convw_ref, convb_ref, lncg_ref, lncb_ref, bpb_ref, n2g_ref, nfg_ref,
                  win_hbm, wpa_hbm, wpb_hbm, wout_hbm, w1_hbm, w2_hbm,
                  o_ref,
                  win_scr, wpa_scr, wpb_scr, wout_scr, w1_scr, w2_scr, stage_scr, stage_sem,
                  h_scr, u_scr, v_scr, vn_scr, sg_scr, aext_scr, c_scr, cn_scr, g0_scr, g1_scr, m_scr,
                  x1_scr, h2_scr, f_scr, sguwm_scr,
                  *, layer, final_norm, n_tiles, tiles_per_seq):
    T, D = h_scr.shape
    bf16 = jnp.bfloat16
    step = pl.program_id(0)

    @pl.when(step == 0)
    def _():
        _stage_weights(layer, [(win_hbm, win_scr, 0), (w1_hbm, w1_scr, None),
                               (wpa_hbm, wpa_scr, None), (wpb_hbm, wpb_scr, None),
                               (wout_hbm, wout_scr, 0), (w2_hbm, w2_scr, None)],
                       stage_scr, stage_sem)
        x1_scr[...] = jnp.zeros(x1_scr.shape, x1_scr.dtype)
        h2_scr[...] = jnp.zeros(h2_scr.shape, h2_scr.dtype)
        row = lax.broadcasted_iota(jnp.int32, (CHUNK, CHUNK), 0)
        col = lax.broadcasted_iota(jnp.int32, (CHUNK, CHUNK), 1)
        for g in range(SGU_GROUPS):
            sguwm_scr[g] = jnp.where(row >= col, sguw_ref[g], 0.0).astype(bf16)

    @pl.when(lax.rem(jnp.minimum(step, n_tiles - 1), tiles_per_seq) == 0)
    def _():
        aext_scr[:, 0:HALO, :] = jnp.zeros((D // LANES, HALO, LANES), jnp.float32)

    def ffn_up(cb):
        c0 = cb * COLS
        f = jnp.maximum(_dot(h2_scr[...], w1_scr[:, c0:c0 + COLS]), 0.0)
        f_scr[:, c0:c0 + COLS] = (f * f).astype(bf16)

    def ffn_down(cb):
        c0 = cb * COLS
        o_ref[:, c0:c0 + COLS] = x1_scr[:, c0:c0 + COLS] + _dot(f_scr[...], w2_scr[:, c0:c0 + COLS])

    def ffn_finish():
        if final_norm:
            o_ref[...] = _rmsnorm(o_ref[...], nfg_ref[...])

    def mixer_start():
        h_scr[...] = _rmsnorm(x_ref[...], n1g_ref[...]).astype(bf16)

    def half_in_proj(col0):
        return _dot(h_scr[...], win_scr[:, col0:col0 + COLS]) + 0.5 * bin_ref[:, col0:col0 + COLS]

    def glu_block(cb):
        c0 = cb * COLS
        half_val = half_in_proj(2 * D + c0)
        a = half_val + half_val * jnp.tanh(half_in_proj(3 * D + c0))
        for j in range(COLS // LANES):
            aext_scr[c0 // LANES + j, HALO:HALO + T, :] = a[:, j * LANES:(j + 1) * LANES]

    def conv_block(lb):
        tap0 = HALO - (CONV_KERNEL - 1)
        l0 = lb * LANES
        for rb in range(T // CONV_ROWS):
            r0 = rb * CONV_ROWS
            acc = jnp.zeros((CONV_ROWS, LANES), jnp.float32) + convb_ref[:, l0:l0 + LANES]
            for k in range(CONV_KERNEL):
                a_k = aext_scr[lb, pl.ds(r0 + tap0 + k, CONV_ROWS, stride=1), :]
                acc = acc + convw_ref[k:k + 1, l0:l0 + LANES] * a_k
            c_scr[r0:r0 + CONV_ROWS, l0:l0 + LANES] = acc

    def gelu_block(dst_scr, seg, cb):
        c0 = cb * COLS
        dst_scr[:, c0:c0 + COLS] = _gelu_tanh_of_twice(half_in_proj(seg * D + c0))

    def gate_block(dst_scr, seg, cb):
        c0 = cb * COLS
        dst_scr[:, c0:c0 + COLS] = jnp.tanh(half_in_proj(seg * D + c0))

    def halo_keep():
        aext_scr[:, 0:HALO, :] = aext_scr[:, T:T + HALO, :]

    def spatial_gating():
        vn_scr[...] = _layernorm(v_scr[...], lnag_ref[...], lnab_ref[...]).astype(bf16)
        n_chunks = T // CHUNK
        for g in range(SGU_GROUPS):
            c0 = g * LANES
            vg = jnp.concatenate(
                [vn_scr[n * CHUNK:(n + 1) * CHUNK, c0:c0 + LANES] for n in range(n_chunks)], axis=1)
            mixed = _dot(sguwm_scr[g], vg)
            for n in range(n_chunks):
                r0 = n * CHUNK
                sg_scr[r0:r0 + CHUNK, c0:c0 + LANES] = (
                    u_scr[r0:r0 + CHUNK, c0:c0 + LANES]
                    * (mixed[:, n * LANES:(n + 1) * LANES] + sgub_ref[g])).astype(bf16)

    def conv_tail():
        half_cn = _layernorm(c_scr[...], 0.5 * lncg_ref[...], 0.5 * lncb_ref[...])
        cn_scr[...] = (half_cn + half_cn * jnp.tanh(half_cn)).astype(bf16)

    def merge():
        y_a = _dot(sg_scr[...], wpa_scr[...])
        y_b = _dot(cn_scr[...], wpb_scr[...]) + bpb_ref[...]
        m_scr[...] = ((y_a + y_b) + (g0_scr[...] * y_a + g1_scr[...] * y_b)).astype(bf16)

    def mixer_finish():
        x1 = x_ref[...] + _dot(m_scr[...], wout_scr[...])
        x1_scr[...] = x1
        h2_scr[...] = _rmsnorm(x1, n2g_ref[...]).astype(bf16)

    n_in = D // COLS
    ffn_work = ([functools.partial(ffn_up, cb) for cb in range(f_scr.shape[1] // COLS)]
                + [functools.partial(ffn_down, cb) for cb in range(n_in)])

    def next_ffn():
        if ffn_work:
            ffn_work.pop(0)()

    next_ffn()
    mixer_start()
    for cb in range(n_in):
        next_ffn()
        glu_block(cb)
    for lb in range(D // LANES):
        conv_block(lb)
        next_ffn()
    while ffn_work:
        next_ffn()
    halo_keep()
    ffn_finish()
    for cb in range(n_in):
        gelu_block(u_scr, 0, cb)
    for cb in range(n_in):
        if cb == n_in - 1:
            conv_tail()
        gelu_block(v_scr, 1, cb)
    gate_block(g0_scr, 4, 0)
    spatial_gating()
    for cb in range(1, n_in):
        gate_block(g0_scr, 4, cb)
    for cb in range(n_in):
        gate_block(g1_scr, 5, cb)
    merge()
    mixer_finish()


def _resident(shape):
    zeros = (0,) * len(shape)
    return pl.BlockSpec(shape, lambda *_: zeros, pipeline_mode=pl.Buffered(1))


def _nbytes(shape, dtype):
    return math.prod(shape) * jnp.dtype(dtype).itemsize


def _vmem_limit(resident, streamed, scratch, temporaries):
    total = sum(_nbytes(*b) for b in resident) + 2 * sum(_nbytes(*b) for b in streamed)
    total += sum(_nbytes(*b) for b in scratch) + sum(_nbytes(*b) for b in temporaries)
    assert total <= VMEM_CAPACITY, total
    return total


def _block(x2d, layer, tiles_per_seq, final_norm, params, weights):
    M, D = x2d.shape
    F = weights[4].shape[2]
    T = TOKENS
    assert M % T == 0 and T % CHUNK == 0 and D % COLS == 0 and F % COLS == 0 and T % CONV_ROWS == 0
    n_tiles = M // T
    f32, bf16 = jnp.float32, jnp.bfloat16
    weight_buffers = [(w.shape[1:], bf16) for w in weights] + [((STAGE_SLOTS, STAGE_ROWS, STAGE_COLS), f32)]
    scratch = [
        ((T, D), bf16),
        ((T, D), f32),
        ((T, D), f32),
        ((T, D), bf16),
        ((T, D), bf16),
        ((D // LANES, HALO + T, LANES), f32),
        ((T, D), f32),
        ((T, D), bf16),
        ((T, D), f32),
        ((T, D), f32),
        ((T, D), bf16),
        ((T, D), f32),
        ((T, D), bf16),
        ((T, F), bf16),
        ((SGU_GROUPS, CHUNK, CHUNK), bf16),
    ]
    limit = _vmem_limit(
        resident=[(p.shape, p.dtype) for p in params],
        streamed=[((T, D), f32), ((T, D), f32)],
        scratch=weight_buffers + scratch,
        temporaries=[((T, D), f32)] * 2)
    return pl.pallas_call(
        functools.partial(_block_kernel, layer=layer, final_norm=final_norm, n_tiles=n_tiles,
                          tiles_per_seq=tiles_per_seq),
        grid=(n_tiles + 1,),
        in_specs=[pl.BlockSpec((T, D), lambda i: (jnp.minimum(i, n_tiles - 1), 0))]
                 + [_resident(p.shape) for p in params]
                 + [pl.BlockSpec(memory_space=pl.ANY)] * len(weights),
        out_specs=pl.BlockSpec((T, D), lambda i: (jnp.maximum(i - 1, 0), 0)),
        out_shape=jax.ShapeDtypeStruct(x2d.shape, x2d.dtype),
        scratch_shapes=[pltpu.VMEM(shape, dtype) for shape, dtype in weight_buffers]
                       + [pltpu.SemaphoreType.DMA((STAGE_SLOTS,))]
                       + [pltpu.VMEM(shape, dtype) for shape, dtype in scratch],
        compiler_params=pltpu.CompilerParams(
            dimension_semantics=("arbitrary",), vmem_limit_bytes=limit),
        name="block",
    )(x2d, *params, *weights)


def kernel(x, norm1_g, w_in, b_in, sgu_ln_g, sgu_ln_b, sgu_w, sgu_b, w_proj_a, conv_w, conv_b,
           conv_ln_g, conv_ln_b, w_proj_b, b_proj_b, w_out, norm2_g, w_ff1, w_ff2, norm_f_g):
    B, S, D = x.shape
    depth = w_in.shape[0]
    row = lambda p: p.reshape(1, -1)
    x2d = x.reshape(B * S, D)
    for l in range(depth):
        sgub = jnp.broadcast_to(sgu_b[l][:, :, None], (SGU_GROUPS, CHUNK, LANES))
        params = (row(norm1_g[l]), row(b_in[l]), row(sgu_ln_g[l]), row(sgu_ln_b[l]), sgu_w[l], sgub,
                  conv_w[l], row(conv_b[l]), row(conv_ln_g[l]), row(conv_ln_b[l]), row(b_proj_b[l]),
                  row(norm2_g[l]), row(norm_f_g))
        x2d = _block(x2d, l, S // TOKENS, l == depth - 1, params,
                     (w_in, w_proj_a, w_proj_b, w_out, w_ff1, w_ff2))
    return x2d.reshape(B, S, D)
```

```python
import functools
import math

import jax
import jax.numpy as jnp
from jax import lax
from jax.experimental import pallas as pl
from jax.experimental.pallas import tpu as pltpu

CHUNK = 128
SGU_GROUPS = 8
CONV_KERNEL = 31
EPS = 1e-6

LANES = 128
HALO = 32
TOKENS = 256
COLS = 1024
FFN_COLS = 2048
CONV_ROWS = 16
STAGE_ROWS, STAGE_COLS = 64, 1024
STAGE_SLOTS = 8
VMEM_CAPACITY = 64 * 1024 * 1024

_GELU_C = math.sqrt(2.0 / math.pi)


def _dot(a, b):
    return jnp.dot(a, b, preferred_element_type=jnp.float32)


def _gelu_tanh_of_twice(half):
    return half + half * jnp.tanh(half * (2.0 * _GELU_C + (8.0 * _GELU_C * 0.044715) * (half * half)))


def _rmsnorm(x, g):
    return x * lax.rsqrt(jnp.mean(x * x, axis=-1, keepdims=True) + EPS) * g


def _layernorm(x, g, b):
    mu = jnp.mean(x, axis=-1, keepdims=True)
    xc = x - mu
    var = jnp.mean(xc * xc, axis=-1, keepdims=True)
    return xc * lax.rsqrt(var + EPS) * g + b


def _stage_weights(layer, pairs, stage_scr, sem):
    chunks = []
    for src, dst, halve_from_col in pairs:
        rows, cols = dst.shape
        assert rows % STAGE_ROWS == 0 and cols % STAGE_COLS == 0
        assert halve_from_col is None or halve_from_col % STAGE_COLS == 0
        for r0 in range(0, rows, STAGE_ROWS):
            for c0 in range(0, cols, STAGE_COLS):
                halve = halve_from_col is not None and c0 >= halve_from_col
                chunks.append((src, dst, r0, c0, halve))

    def copy(j):
        src, _, r0, c0, _ = chunks[j]
        return pltpu.make_async_copy(
            src.at[layer, pl.ds(r0, STAGE_ROWS), pl.ds(c0, STAGE_COLS)],
            stage_scr.at[j % STAGE_SLOTS], sem.at[j % STAGE_SLOTS])

    lookahead = STAGE_SLOTS - 1
    for j in range(min(lookahead, len(chunks))):
        copy(j).start()
    for j, (_, dst, r0, c0, halve) in enumerate(chunks):
        if j + lookahead < len(chunks):
            copy(j + lookahead).start()
        copy(j).wait()
        w = stage_scr[j % STAGE_SLOTS]
        dst[r0:r0 + STAGE_ROWS, c0:c0 + STAGE_COLS] = (0.5 * w if halve else w).astype(jnp.bfloat16)


def _block_kernel(x_ref, n1g_ref, bin_ref, lnag_ref, lnab_ref, sguw_ref, sgub_ref,
                  convw_ref, convb_ref, lncg_ref, lncb_ref, bpb_ref, n2g_ref, nfg_ref,
                  win_hbm, wpa_hbm, wpb_hbm, wout_hbm, w1_hbm, w2_hbm,
                  o_ref,
                  win_scr, wpa_scr, wpb_scr, wout_scr, w1_scr, w2_scr, stage_scr, stage_sem,
                  h_scr, u_scr, v_scr, vn_scr, sg_scr, aext_scr, c_scr, cn_scr, g0_scr, g1_scr, m_scr,
                  x1_scr, h2_scr, f_scr, sguwm_scr,
                  *, layer, final_norm, n_tiles, tiles_per_seq):
    T, D = h_scr.shape
    bf16 = jnp.bfloat16
    step = pl.program_id(0)

    @pl.when(step == 0)
    def _():
        _stage_weights(layer, [(win_hbm, win_scr, 0), (w1_hbm, w1_scr, None),
                               (wpa_hbm, wpa_scr, None), (wpb_hbm, wpb_scr, None),
                               (wout_hbm, wout_scr, 0), (w2_hbm, w2_scr, None)],
                       stage_scr, stage_sem)
        x1_scr[...] = jnp.zeros(x1_scr.shape, x1_scr.dtype)
        h2_scr[...] = jnp.zeros(h2_scr.shape, h2_scr.dtype)
        row = lax.broadcasted_iota(jnp.int32, (CHUNK, CHUNK), 0)
        col = lax.broadcasted_iota(jnp.int32, (CHUNK, CHUNK), 1)
        for g in range(SGU_GROUPS):
            sguwm_scr[g] = jnp.where(row >= col, sguw_ref[g], 0.0).astype(bf16)

    @pl.when(lax.rem(jnp.minimum(step, n_tiles - 1), tiles_per_seq) == 0)
    def _():
        aext_scr[:, 0:HALO, :] = jnp.zeros((D // LANES, HALO, LANES), jnp.float32)

    def ffn_up(cb):
        c0 = cb * FFN_COLS
        f = jnp.maximum(_dot(h2_scr[...], w1_scr[:, c0:c0 + FFN_COLS]), 0.0)
        f_scr[:, c0:c0 + FFN_COLS] = (f * f).astype(bf16)

    def ffn_down(cb):
        c0 = cb * COLS
        o_ref[:, c0:c0 + COLS] = x1_scr[:, c0:c0 + COLS] + _dot(f_scr[...], w2_scr[:, c0:c0 + COLS])

    def ffn_finish():
        if final_norm:
            o_ref[...] = _rmsnorm(o_ref[...], nfg_ref[...])

    def mixer_start():
        h_scr[...] = _rmsnorm(x_ref[...], n1g_ref[...]).astype(bf16)

    def half_in_proj(col0):
        return _dot(h_scr[...], win_scr[:, col0:col0 + COLS]) + 0.5 * bin_ref[:, col0:col0 + COLS]

    def glu_block(cb):
        c0 = cb * COLS
        half_val = half_in_proj(2 * D + c0)
        a = half_val + half_val * jnp.tanh(half_in_proj(3 * D + c0))
        for j in range(COLS // LANES):
            aext_scr[c0 // LANES + j, HALO:HALO + T, :] = a[:, j * LANES:(j + 1) * LANES]

    def conv_block(lb):
        tap0 = HALO - (CONV_KERNEL - 1)
        l0 = lb * LANES
        for rb in range(T // CONV_ROWS):
            r0 = rb * CONV_ROWS
            acc = jnp.zeros((CONV_ROWS, LANES), jnp.float32) + convb_ref[:, l0:l0 + LANES]
            for k in range(CONV_KERNEL):
                a_k = aext_scr[lb, pl.ds(r0 + tap0 + k, CONV_ROWS, stride=1), :]
                acc = acc + convw_ref[k:k + 1, l0:l0 + LANES] * a_k
            c_scr[r0:r0 + CONV_ROWS, l0:l0 + LANES] = acc

    def gelu_block(dst_scr, seg, cb):
        c0 = cb * COLS
        dst_scr[:, c0:c0 + COLS] = _gelu_tanh_of_twice(half_in_proj(seg * D + c0))

    def gate_block(dst_scr, seg, cb):
        c0 = cb * COLS
        dst_scr[:, c0:c0 + COLS] = jnp.tanh(half_in_proj(seg * D + c0))

    def halo_keep():
        aext_scr[:, 0:HALO, :] = aext_scr[:, T:T + HALO, :]

    def spatial_gating():
        vn_scr[...] = _layernorm(v_scr[...], lnag_ref[...], lnab_ref[...]).astype(bf16)
        n_chunks = T // CHUNK
        for g in range(SGU_GROUPS):
            c0 = g * LANES
            vg = jnp.concatenate(
                [vn_scr[n * CHUNK:(n + 1) * CHUNK, c0:c0 + LANES] for n in range(n_chunks)], axis=1)
            mixed = _dot(sguwm_scr[g], vg)
            for n in range(n_chunks):
                r0 = n * CHUNK
                sg_scr[r0:r0 + CHUNK, c0:c0 + LANES] = (
                    u_scr[r0:r0 + CHUNK, c0:c0 + LANES]
                    * (mixed[:, n * LANES:(n + 1) * LANES] + sgub_ref[g])).astype(bf16)

    def conv_tail():
        half_cn = _layernorm(c_scr[...], 0.5 * lncg_ref[...], 0.5 * lncb_ref[...])
        cn_scr[...] = (half_cn + half_cn * jnp.tanh(half_cn)).astype(bf16)

    def merge():
        y_a = _dot(sg_scr[...], wpa_scr[...])
        y_b = _dot(cn_scr[...], wpb_scr[...]) + bpb_ref[...]
        m_scr[...] = ((y_a + y_b) + (g0_scr[...] * y_a + g1_scr[...] * y_b)).astype(bf16)

    def mixer_finish():
        x1 = x_ref[...] + _dot(m_scr[...], wout_scr[...])
        x1_scr[...] = x1
        h2_scr[...] = _rmsnorm(x1, n2g_ref[...]).astype(bf16)

    n_in = D // COLS
    ffn_work = ([functools.partial(ffn_up, cb) for cb in range(f_scr.shape[1] // FFN_COLS)]
                + [functools.partial(ffn_down, cb) for cb in range(n_in)])

    def next_ffn():
        if ffn_work:
            ffn_work.pop(0)()

    next_ffn()
    mixer_start()
    for cb in range(n_in):
        next_ffn()
        glu_block(cb)
    for lb in range(D // LANES):
        conv_block(lb)
        next_ffn()
    while ffn_work:
        next_ffn()
    halo_keep()
    ffn_finish()
    for cb in range(n_in):
        gelu_block(u_scr, 0, cb)
    for cb in range(n_in):
        if cb == n_in - 1:
            conv_tail()
        gelu_block(v_scr, 1, cb)
    gate_block(g0_scr, 4, 0)
    spatial_gating()
    for cb in range(1, n_in):
        gate_block(g0_scr, 4, cb)
    for cb in range(n_in):
        gate_block(g1_scr, 5, cb)
    merge()
    mixer_finish()


def _resident(shape):
    zeros = (0,) * len(shape)
    return pl.BlockSpec(shape, lambda *_: zeros, pipeline_mode=pl.Buffered(1))


def _nbytes(shape, dtype):
    return math.prod(shape) * jnp.dtype(dtype).itemsize


def _vmem_limit(resident, streamed, scratch, temporaries):
    total = sum(_nbytes(*b) for b in resident) + 2 * sum(_nbytes(*b) for b in streamed)
    total += sum(_nbytes(*b) for b in scratch) + sum(_nbytes(*b) for b in temporaries)
    assert total <= VMEM_CAPACITY, total
    return total


def _block(x2d, layer, tiles_per_seq, final_norm, params, weights):
    M, D = x2d.shape
    F = weights[4].shape[2]
    T = TOKENS
    assert M % T == 0 and T % CHUNK == 0 and D % COLS == 0 and F % FFN_COLS == 0 and T % CONV_ROWS == 0
    n_tiles = M // T
    f32, bf16 = jnp.float32, jnp.bfloat16
    weight_buffers = [(w.shape[1:], bf16) for w in weights] + [((STAGE_SLOTS, STAGE_ROWS, STAGE_COLS), f32)]
    scratch = [
        ((T, D), bf16),
        ((T, D), f32),
        ((T, D), f32),
        ((T, D), bf16),
        ((T, D), bf16),
        ((D // LANES, HALO + T, LANES), f32),
        ((T, D), f32),
        ((T, D), bf16),
        ((T, D), f32),
        ((T, D), f32),
        ((T, D), bf16),
        ((T, D), f32),
        ((T, D), bf16),
        ((T, F), bf16),
        ((SGU_GROUPS, CHUNK, CHUNK), bf16),
    ]
    limit = _vmem_limit(
        resident=[(p.shape, p.dtype) for p in params],
        streamed=[((T, D), f32), ((T, D), f32)],
        scratch=weight_buffers + scratch,
        temporaries=[((T, D), f32)] * 2)
    return pl.pallas_call(
        functools.partial(_block_kernel, layer=layer, final_norm=final_norm, n_tiles=n_tiles,
                          tiles_per_seq=tiles_per_seq),
        grid=(n_tiles + 1,),
        in_specs=[pl.BlockSpec((T, D), lambda i: (jnp.minimum(i, n_tiles - 1), 0))]
                 + [_resident(p.shape) for p in params]
                 + [pl.BlockSpec(memory_space=pl.ANY)] * len(weights),
        out_specs=pl.BlockSpec((T, D), lambda i: (jnp.maximum(i - 1, 0), 0)),
        out_shape=jax.ShapeDtypeStruct(x2d.shape, x2d.dtype),
        scratch_shapes=[pltpu.VMEM(shape, dtype) for shape, dtype in weight_buffers]
                       + [pltpu.SemaphoreType.DMA((STAGE_SLOTS,))]
                       + [pltpu.VMEM(shape, dtype) for shape, dtype in scratch],
        compiler_params=pltpu.CompilerParams(
            dimension_semantics=("arbitrary",), vmem_limit_bytes=limit),
        name="block",
    )(x2d, *params, *weights)


def kernel(x, norm1_g, w_in, b_in, sgu_ln_g, sgu_ln_b, sgu_w, sgu_b, w_proj_a, conv_w, conv_b,
           conv_ln_g, conv_ln_b, w_proj_b, b_proj_b, w_out, norm2_g, w_ff1, w_ff2, norm_f_g):
    B, S, D = x.shape
    depth = w_in.shape[0]
    row = lambda p: p.reshape(1, -1)
    x2d = x.reshape(B * S, D)
    for l in range(depth):
        sgub = jnp.broadcast_to(sgu_b[l][:, :, None], (SGU_GROUPS, CHUNK, LANES))
        params = (row(norm1_g[l]), row(b_in[l]), row(sgu_ln_g[l]), row(sgu_ln_b[l]), sgu_w[l], sgub,
                  conv_w[l], row(conv_b[l]), row(conv_ln_g[l]), row(conv_ln_b[l]), row(b_proj_b[l]),
                  row(norm2_g[l]), row(norm_f_g))
        x2d = _block(x2d, l, S // TOKENS, l == depth - 1, params,
                     (w_in, w_proj_a, w_proj_b, w_out, w_ff1, w_ff2))
    return x2d.reshape(B, S, D)
```
